```python
import jax, jax.numpy as jnp
from jax import lax
import numpy as np

D_MODEL = 4096
BATCH = 4
SEQ = 2048
DEPTH = 2
DEC_BATCH = 8
DEC_SEQ = 8
PAST_LEN = 16384
PAGE_SIZE = 128

SB_HEADS = 16
SB_HEAD_DIM = 128
D_ATTN = SB_HEADS * SB_HEAD_DIM
Q_BLOCK = 128
SB_BIAS_INIT = -8.0
D_SSM = D_MODEL
SSM_HEAD_DIM = 64
SSM_HEADS = D_SSM // SSM_HEAD_DIM
SSM_GROUPS = 8
SSM_HEADS_PER_GROUP = SSM_HEADS // SSM_GROUPS
SSM_STATE = 128
CONV_WIDTH = 4
CONV_CH = D_SSM + 2 * SSM_GROUPS * SSM_STATE
SSD_CHUNK = 128
D_FF = 11008
IN_COLS = 3 * D_ATTN + D_SSM + CONV_CH + SSM_HEADS
N_MOD = 9
EPS = 1e-6

kernel_name = 'hybrid_stickbreak_ssd_macaron_adaln_step'


def rms_norm(x, g):
    xf = x.astype(jnp.float32)
    y = xf * lax.rsqrt(jnp.mean(xf * xf, axis=-1, keepdims=True) + EPS)
    return (y * g.astype(jnp.float32)).astype(x.dtype)


def modulate(x, g, shift, scale):
    return rms_norm(x, g) * (1.0 + scale[:, None, :]) + shift[:, None, :]


def swiglu(n, w_in, w_out):
    gate, up = jnp.split(n @ w_in, 2, axis=-1)
    return (jax.nn.silu(gate) * up) @ w_out


def stick_breaking_block(q, k, v, q_pos, k_pos, sb_bias):
    z = jnp.einsum('bqhd,bkhd->bhqk', q, k).astype(jnp.float32) * (SB_HEAD_DIM ** -0.5)
    z = z + sb_bias.astype(jnp.float32)[None, :, None, None]
    visible = k_pos[None, :] < q_pos[:, None]
    log_keep = jnp.where(visible, jax.nn.log_sigmoid(-z), 0.0)
    later = lax.cumsum(log_keep, axis=3, reverse=True) - log_keep
    w = jnp.where(visible, jnp.exp(jax.nn.log_sigmoid(z) + later), 0.0)
    return jnp.einsum('bhqk,bkhd->bqhd', w.astype(v.dtype), v)


def stick_breaking_attention(q, k_all, v_all, past_len, sb_bias):
    T = q.shape[1]
    outs = []
    for start in range(0, T, Q_BLOCK):
        stop = min(start + Q_BLOCK, T)
        k_end = past_len + stop
        q_pos = past_len + jnp.arange(start, stop)
        k_pos = jnp.arange(k_end)
        outs.append(stick_breaking_block(q[:, start:stop], k_all[:, :k_end], v_all[:, :k_end], q_pos, k_pos,
                                         sb_bias))
    return jnp.concatenate(outs, axis=1)


def causal_conv(xbc, buf, w, b):
    T = xbc.shape[1]
    xp = jnp.concatenate([buf, xbc], axis=1)
    out = b + xp[:, 0:T] * w[0]
    for i in range(1, CONV_WIDTH):
        out = out + xp[:, i:i + T] * w[i]
    return jax.nn.silu(out), xp[:, T:]


def pad_time(t, pad):
    return jnp.pad(t, [(0, 0), (0, pad)] + [(0, 0)] * (t.ndim - 2))


def ssd_scan(x, dt, a, bm, cm, h0):
    f32 = jnp.float32
    bsz, T = x.shape[:2]
    q = min(SSD_CHUNK, T)
    pad = (-T) % q
    x, dt, bm, cm = x.astype(f32), dt.astype(f32), bm.astype(f32), cm.astype(f32)
    if pad:
        x, dt, bm, cm = pad_time(x, pad), pad_time(dt, pad), pad_time(bm, pad), pad_time(cm, pad)
    nc = (T + pad) // q
    G, J = SSM_GROUPS, SSM_HEADS_PER_GROUP
    x = x.reshape(bsz, nc, q, G, J, SSM_HEAD_DIM)
    dt = dt.reshape(bsz, nc, q, G, J)
    bm = bm.reshape(bsz, nc, q, G, SSM_STATE)
    cm = cm.reshape(bsz, nc, q, G, SSM_STATE)
    a_cum = jnp.cumsum(dt * a.reshape(G, J), axis=2)
    xdt = x * dt[..., None]
    causal = jnp.tril(jnp.ones((q, q), bool))[:, :, None, None]
    seg = a_cum[:, :, :, None] - a_cum[:, :, None, :]
    decay_ls = jnp.exp(jnp.where(causal, seg, -jnp.inf))
    cb = jnp.einsum('bclgn,bcsgn->bclsg', cm, bm)
    y_diag = jnp.einsum('bclsg,bclsgj,bcsgjp->bclgjp', cb, decay_ls, xdt)
    decay_end = jnp.exp(a_cum[:, :, -1:] - a_cum)
    chunk_states = jnp.einsum('bcsgn,bcsgj,bcsgjp->bcgjpn', bm, decay_end, xdt)
    chunk_decay = jnp.exp(a_cum[:, :, -1])

    def step(h, inp):
        st, dec = inp
        return h * dec[..., None, None] + st, h

    h_init = h0.astype(f32).reshape(bsz, G, J, SSM_HEAD_DIM, SSM_STATE)
    h_final, h_start = lax.scan(step, h_init, (jnp.moveaxis(chunk_states, 1, 0), jnp.moveaxis(chunk_decay, 1, 0)))
    h_start = jnp.moveaxis(h_start, 0, 1)
    y_off = jnp.einsum('bclgn,bcgjpn,bclgj->bclgjp', cm, h_start, jnp.exp(a_cum))
    y = (y_diag + y_off).reshape(bsz, nc * q, SSM_HEADS, SSM_HEAD_DIM)[:, :T]
    return y, h_final.reshape(bsz, SSM_HEADS, SSM_HEAD_DIM, SSM_STATE)


def mamba2_branch(z, xbc, dt_raw, conv_buf, h0, conv_w, conv_b, dt_bias, a_log, d_skip, ssm_norm_g):
    f32 = jnp.float32
    bsz, T = z.shape[:2]
    xbc, conv_new = causal_conv(xbc, conv_buf, conv_w, conv_b)
    xs, bm, cm = jnp.split(xbc, [D_SSM, D_SSM + SSM_GROUPS * SSM_STATE], axis=-1)
    xs = xs.reshape(bsz, T, SSM_HEADS, SSM_HEAD_DIM)
    bm = bm.reshape(bsz, T, SSM_GROUPS, SSM_STATE)
    cm = cm.reshape(bsz, T, SSM_GROUPS, SSM_STATE)
    dt = jax.nn.softplus(dt_raw.astype(f32) + dt_bias.astype(f32))
    a = -jnp.exp(a_log.astype(f32))
    y, h_new = ssd_scan(xs, dt, a, bm, cm, h0)
    y = y + xs.astype(f32) * d_skip.astype(f32)[:, None]
    y = y.reshape(bsz, T, D_SSM) * jax.nn.silu(z.astype(f32))
    yg = y.reshape(bsz, T, SSM_GROUPS, D_SSM // SSM_GROUPS)
    yg = yg * lax.rsqrt(jnp.mean(yg * yg, axis=-1, keepdims=True) + EPS)
    y = yg.reshape(bsz, T, D_SSM) * ssm_norm_g.astype(f32)
    return y.astype(z.dtype), h_new.astype(h0.dtype), conv_new


def decoder_layer(x, c, k_past, v_past, h0, conv_buf, norm_g, w_mod, b_mod, w_ffn1_in, w_ffn1_out, w_in, sb_bias,
                  conv_w, conv_b, dt_bias, a_log, d_skip, ssm_norm_g, w_gate, b_gate, w_branch_attn,
                  w_branch_ssm, w_out, w_ffn2_in, w_ffn2_out):
    bsz, T, _ = x.shape
    past_len = k_past.shape[1]
    mod = jax.nn.silu(c) @ w_mod + b_mod
    sh1, sc1, g1, sh2, sc2, g2, sh3, sc3, g3 = jnp.split(mod, N_MOD, axis=-1)
    n = modulate(x, norm_g[0], sh1, sc1)
    x = x + 0.5 * g1[:, None, :] * swiglu(n, w_ffn1_in, w_ffn1_out)
    n = modulate(x, norm_g[1], sh2, sc2)
    proj = n @ w_in
    q, k, v, z, xbc, dt_raw = jnp.split(
        proj, [D_ATTN, 2 * D_ATTN, 3 * D_ATTN, 3 * D_ATTN + D_SSM, 3 * D_ATTN + D_SSM + CONV_CH], axis=-1)
    q = q.reshape(bsz, T, SB_HEADS, SB_HEAD_DIM)
    k = k.reshape(bsz, T, SB_HEADS, SB_HEAD_DIM)
    v = v.reshape(bsz, T, SB_HEADS, SB_HEAD_DIM)
    k_all = jnp.concatenate([k_past, k], axis=1)
    v_all = jnp.concatenate([v_past, v], axis=1)
    o_attn = stick_breaking_attention(q, k_all, v_all, past_len, sb_bias).reshape(bsz, T, D_ATTN)
    o_ssm, h_new, conv_new = mamba2_branch(z, xbc, dt_raw, conv_buf, h0, conv_w, conv_b, dt_bias, a_log,
                                           d_skip, ssm_norm_g)
    g_attn, g_ssm = jnp.split(jax.nn.sigmoid(n @ w_gate + b_gate), 2, axis=-1)
    merged = g_attn * (o_attn @ w_branch_attn) + g_ssm * (o_ssm @ w_branch_ssm)
    x = x + g2[:, None, :] * (merged @ w_out)
    n = modulate(x, norm_g[2], sh3, sc3)
    x = x + 0.5 * g3[:, None, :] * swiglu(n, w_ffn2_in, w_ffn2_out)
    return x, k, v, h_new, conv_new


def setup_inputs(seed: int = 0) -> dict:
    key = jax.random.key(seed)
    ks = jax.random.split(key, 32)
    f32 = jnp.float32

    def nrm(k, shape, scale):
        return jax.random.normal(k, shape, f32) * scale

    n_pages = PAST_LEN // PAGE_SIZE
    n_used = DEC_BATCH * n_pages
    n_pool = n_used + n_used // 4
    page_table = jax.random.permutation(ks[6], n_pool)[:n_used].reshape(DEC_BATCH, n_pages).astype(jnp.int32)
    dt0 = jnp.exp(jax.random.uniform(ks[16], (DEPTH, SSM_HEADS), f32) * (np.log(0.1) - np.log(0.001)) + np.log(0.001))
    dt_bias = dt0 + jnp.log(-jnp.expm1(-dt0))
    a_log = jnp.log(jax.random.uniform(ks[17], (DEPTH, SSM_HEADS), f32, 1.0, 16.0))
    return {
        'x_prompt': nrm(ks[0], (BATCH, SEQ, D_MODEL), 1.0),
        'x_sample': nrm(ks[1], (DEC_BATCH, DEC_SEQ, D_MODEL), 1.0),
        'c_prompt': nrm(ks[2], (BATCH, D_MODEL), 1.0),
        'c_sample': nrm(ks[3], (DEC_BATCH, D_MODEL), 1.0),
        'cache_k': nrm(ks[4], (DEPTH, n_pool, PAGE_SIZE, SB_HEADS, SB_HEAD_DIM), 1.0),
        'cache_v': nrm(ks[5], (DEPTH, n_pool, PAGE_SIZE, SB_HEADS, SB_HEAD_DIM), 1.0),
        'page_table': page_table,
        'state_ssm': nrm(ks[7], (DEPTH, DEC_BATCH, SSM_HEADS, SSM_HEAD_DIM, SSM_STATE), 0.5),
        'state_conv': nrm(ks[8], (DEPTH, DEC_BATCH, CONV_WIDTH - 1, CONV_CH), 1.0),
        'norm_g': 1.0 + nrm(ks[9], (DEPTH, 3, D_MODEL), 0.02),
        'w_mod': nrm(ks[10], (DEPTH, D_MODEL, N_MOD * D_MODEL), 0.5 * D_MODEL ** -0.5),
        'b_mod': nrm(ks[11], (DEPTH, N_MOD * D_MODEL), 0.02),
        'w_ffn1_in': nrm(ks[12], (DEPTH, D_MODEL, 2 * D_FF), D_MODEL ** -0.5),
        'w_ffn1_out': nrm(ks[13], (DEPTH, D_FF, D_MODEL), D_FF ** -0.5),
        'w_in': nrm(ks[14], (DEPTH, D_MODEL, IN_COLS), D_MODEL ** -0.5),
        'sb_bias': SB_BIAS_INIT + nrm(ks[29], (DEPTH, SB_HEADS), 0.5),
        'conv_w': nrm(ks[15], (DEPTH, CONV_WIDTH, CONV_CH), CONV_WIDTH ** -0.5),
        'conv_b': nrm(ks[18], (DEPTH, CONV_CH), 0.02),
        'dt_bias': dt_bias,
        'a_log': a_log,
        'd_skip': 1.0 + nrm(ks[19], (DEPTH, SSM_HEADS), 0.02),
        'ssm_norm_g': 1.0 + nrm(ks[20], (DEPTH, D_SSM), 0.02),
        'w_gate': nrm(ks[21], (DEPTH, D_MODEL, 2 * D_MODEL), D_MODEL ** -0.5),
        'b_gate': nrm(ks[22], (DEPTH, 2 * D_MODEL), 0.02),
        'w_branch_attn': nrm(ks[23], (DEPTH, D_ATTN, D_MODEL), D_ATTN ** -0.5),
        'w_branch_ssm': nrm(ks[24], (DEPTH, D_SSM, D_MODEL), D_SSM ** -0.5),
        'w_out': nrm(ks[25], (DEPTH, D_MODEL, D_MODEL), D_MODEL ** -0.5),
        'w_ffn2_in': nrm(ks[26], (DEPTH, D_MODEL, 2 * D_FF), D_MODEL ** -0.5),
        'w_ffn2_out': nrm(ks[27], (DEPTH, D_FF, D_MODEL), D_FF ** -0.5),
        'final_norm_g': 1.0 + nrm(ks[28], (D_MODEL,), 0.02),
    }


def reference(x_prompt, x_sample, c_prompt, c_sample, cache_k, cache_v, page_table, state_ssm, state_conv,
              norm_g, w_mod, b_mod, w_ffn1_in, w_ffn1_out, w_in, sb_bias, conv_w, conv_b, dt_bias, a_log, d_skip,
              ssm_norm_g, w_gate, b_gate, w_branch_attn, w_branch_ssm, w_out, w_ffn2_in, w_ffn2_out,
              final_norm_g):
    n_prompt = x_prompt.shape[0]
    n_sample = x_sample.shape[0]
    past_len = page_table.shape[1] * cache_k.shape[2]
    empty_kv = jnp.zeros((n_prompt, 0, SB_HEADS, SB_HEAD_DIM), x_prompt.dtype)
    h0_prompt = jnp.zeros((n_prompt, SSM_HEADS, SSM_HEAD_DIM, SSM_STATE), state_ssm.dtype)
    conv0_prompt = jnp.zeros((n_prompt, CONV_WIDTH - 1, CONV_CH), state_conv.dtype)
    xp, xs = x_prompt, x_sample
    kp_list, vp_list, hp_list, cp_list = [], [], [], []
    ks_list, vs_list, hs_list, cs_list = [], [], [], []
    for l in range(DEPTH):
        weights = (norm_g[l], w_mod[l], b_mod[l], w_ffn1_in[l], w_ffn1_out[l], w_in[l], sb_bias[l], conv_w[l],
                   conv_b[l], dt_bias[l], a_log[l], d_skip[l], ssm_norm_g[l], w_gate[l], b_gate[l],
                   w_branch_attn[l], w_branch_ssm[l], w_out[l], w_ffn2_in[l], w_ffn2_out[l])
        xp, k_new, v_new, h_new, conv_new = decoder_layer(xp, c_prompt, empty_kv, empty_kv, h0_prompt,
                                                          conv0_prompt, *weights)
        kp_list.append(k_new); vp_list.append(v_new); hp_list.append(h_new); cp_list.append(conv_new)
        k_past = cache_k[l][page_table].reshape(n_sample, past_len, SB_HEADS, SB_HEAD_DIM)
        v_past = cache_v[l][page_table].reshape(n_sample, past_len, SB_HEADS, SB_HEAD_DIM)
        xs, k_new, v_new, h_new, conv_new = decoder_layer(xs, c_sample, k_past, v_past, state_ssm[l],
                                                          state_conv[l], *weights)
        ks_list.append(k_new); vs_list.append(v_new); hs_list.append(h_new); cs_list.append(conv_new)
    y_prompt = rms_norm(xp, final_norm_g)
    y_sample = rms_norm(xs, final_norm_g)
    k_prompt = jnp.stack(kp_list)
    v_prompt = jnp.stack(vp_list)
    ssm_prompt = jnp.stack(hp_list)
    conv_prompt = jnp.stack(cp_list)
    k_sample = jnp.stack(ks_list)
    v_sample = jnp.stack(vs_list)
    ssm_sample = jnp.stack(hs_list)
    conv_sample = jnp.stack(cs_list)
    return (y_prompt, y_sample, k_prompt, v_prompt, ssm_prompt, conv_prompt, k_sample, v_sample, ssm_sample, conv_sample)
```

```python
import functools

import jax
import jax.numpy as jnp
from jax import lax
from jax.experimental import pallas as pl
from jax.experimental.pallas import tpu as pltpu

F32 = jnp.float32
BF16 = jnp.bfloat16

SB_HEADS = 16
SB_HEAD_DIM = 128
D_ATTN = SB_HEADS * SB_HEAD_DIM
SSM_HEAD_DIM = 64
HEAD_DIM_SHIFT = 6
SSM_GROUPS = 8
SSM_STATE = 128
CONV_WIDTH = 4
N_MOD = 9
EPS = 1e-6
CHUNK = 128
LANES = 128
SUBLANES = 8
VMEM_LIMIT = 56 * 1024 * 1024


def _params(*sem):
    return pltpu.CompilerParams(dimension_semantics=sem, vmem_limit_bytes=VMEM_LIMIT)


def _sigmoid(x):
    return 1.0 / (1.0 + jnp.exp(-x))


def _silu(x):
    return x * _sigmoid(x)


def _softplus(x):
    return jnp.maximum(x, 0.0) + jnp.log1p(jnp.exp(-jnp.abs(x)))


def _bdot(a, b):
    return jnp.dot(a.astype(BF16), b.astype(BF16), preferred_element_type=F32)


def _bdot_nt(a, b):
    return lax.dot_general(a.astype(BF16), b.astype(BF16), (((1,), (1,)), ((), ())),
                           preferred_element_type=F32)


def _split3(x):
    hi = x.astype(BF16)
    r1 = x - hi.astype(F32)
    mid = r1.astype(BF16)
    lo = (r1 - mid.astype(F32)).astype(BF16)
    return hi, mid, lo


def _exact_dot(x, m01):
    hi, mid, lo = _split3(x)
    d = functools.partial(jnp.dot, preferred_element_type=F32)
    return d(hi, m01) + d(mid, m01) + d(lo, m01)


def _mm_bias_kernel(x_ref, w_ref, b_ref, o_ref, *, pre_silu):
    x = x_ref[...]
    if pre_silu:
        x = _silu(x.astype(F32))
    acc = _bdot(x, w_ref[...])
    o_ref[...] = (acc + b_ref[...]).astype(o_ref.dtype)


def _mm_bias(x, w, b, *, col0, n, tm, tn, out_dtype, pre_silu=False):
    m, k = x.shape
    c0 = col0 // tn
    return pl.pallas_call(
        functools.partial(_mm_bias_kernel, pre_silu=pre_silu),
        grid=(m // tm, n // tn),
        in_specs=[pl.BlockSpec((tm, k), lambda i, j: (i, 0)),
                  pl.BlockSpec((k, tn), lambda i, j: (0, c0 + j)),
                  pl.BlockSpec((1, tn), lambda i, j: (0, c0 + j))],
        out_specs=pl.BlockSpec((tm, tn), lambda i, j: (i, j)),
        out_shape=jax.ShapeDtypeStruct((m, n), out_dtype),
        compiler_params=_params("parallel", "arbitrary"),
    )(x, w, b)


def _mm_sigmoid_kernel(x_ref, w_ref, b_ref, o_ref):
    acc = _bdot(x_ref[...], w_ref[...])
    o_ref[...] = _sigmoid(acc + b_ref[...]).astype(o_ref.dtype)


def _mm_sigmoid(x, w, b, *, tm, tn):
    m, k = x.shape
    n = w.shape[1]
    return pl.pallas_call(
        _mm_sigmoid_kernel,
        grid=(m // tm, n // tn),
        in_specs=[pl.BlockSpec((tm, k), lambda i, j: (i, 0)),
                  pl.BlockSpec((k, tn), lambda i, j: (0, j)),
                  pl.BlockSpec((1, tn), lambda i, j: (0, j))],
        out_specs=pl.BlockSpec((tm, tn), lambda i, j: (i, j)),
        out_shape=jax.ShapeDtypeStruct((m, n), BF16),
        compiler_params=_params("parallel", "arbitrary"),
    )(x, w, b)


def _mm_swiglu_kernel(x_ref, wg_ref, wu_ref, o_ref):
    x = x_ref[...]
    gate = _bdot(x, wg_ref[...])
    up = _bdot(x, wu_ref[...])
    o_ref[...] = (_silu(gate) * up).astype(o_ref.dtype)


def _mm_swiglu(x, w, *, tm, tn):
    m, k = x.shape
    f = w.shape[1] // 2
    nb = f // tn
    return pl.pallas_call(
        _mm_swiglu_kernel,
        grid=(m // tm, nb),
        in_specs=[pl.BlockSpec((tm, k), lambda i, j: (i, 0)),
                  pl.BlockSpec((k, tn), lambda i, j: (0, j)),
                  pl.BlockSpec((k, tn), lambda i, j: (0, nb + j))],
        out_specs=pl.BlockSpec((tm, tn), lambda i, j: (i, j)),
        out_shape=jax.ShapeDtypeStruct((m, f), BF16),
        compiler_params=_params("parallel", "arbitrary"),
    )(x, w, w)


def _mm_resid_kernel(x_ref, w_ref, r_ref, g_ref, o_ref, *, scale):
    acc = _bdot(x_ref[...], w_ref[...])
    o_ref[...] = r_ref[...] + (scale * g_ref[0]) * acc


def _mm_resid(x, w, res, gate, *, scale, tm, tn, rows_per_gate):
    m, k = x.shape
    n = w.shape[1]
    r = gate.shape[1]
    if r == 1:
        assert rows_per_gate % tm == 0
        gmap = lambda i, j: ((i * tm) // rows_per_gate, 0, j)
    else:
        assert r == tm
        gmap = lambda i, j: (i, 0, j)
    return pl.pallas_call(
        functools.partial(_mm_resid_kernel, scale=scale),
        grid=(m // tm, n // tn),
        in_specs=[pl.BlockSpec((tm, k), lambda i, j: (i, 0)),
                  pl.BlockSpec((k, tn), lambda i, j: (0, j)),
                  pl.BlockSpec((tm, tn), lambda i, j: (i, j)),
                  pl.BlockSpec((1, r, tn), gmap)],
        out_specs=pl.BlockSpec((tm, tn), lambda i, j: (i, j)),
        out_shape=jax.ShapeDtypeStruct((m, n), F32),
        compiler_params=_params("parallel", "arbitrary"),
    )(x, w, res, gate)


def _mm_merge_kernel(oa_ref, wa_ref, os_ref, ws_ref, ga_ref, gs_ref, o_ref):
    a = _bdot(oa_ref[...], wa_ref[...])
    s = _bdot(os_ref[...], ws_ref[...])
    o_ref[...] = (ga_ref[...].astype(F32) * a + gs_ref[...].astype(F32) * s).astype(o_ref.dtype)


def _mm_merge(oa, wa, os_, ws, gates, *, tm, tn):
    m = oa.shape[0]
    n = wa.shape[1]
    nb = n // tn
    return pl.pallas_call(
        _mm_merge_kernel,
        grid=(m // tm, nb),
        in_specs=[pl.BlockSpec((tm, oa.shape[1]), lambda i, j: (i, 0)),
                  pl.BlockSpec((wa.shape[0], tn), lambda i, j: (0, j)),
                  pl.BlockSpec((tm, os_.shape[1]), lambda i, j: (i, 0)),
                  pl.BlockSpec((ws.shape[0], tn), lambda i, j: (0, j)),
                  pl.BlockSpec((tm, tn), lambda i, j: (i, j)),
                  pl.BlockSpec((tm, tn), lambda i, j: (i, nb + j))],
        out_specs=pl.BlockSpec((tm, tn), lambda i, j: (i, j)),
        out_shape=jax.ShapeDtypeStruct((m, n), BF16),
        compiler_params=_params("parallel", "arbitrary"),
    )(oa, wa, os_, ws, gates, gates)


def _modnorm_kernel(x_ref, g_ref, sh_ref, sc_ref, o_ref):
    x = x_ref[...]
    y = x * lax.rsqrt(jnp.mean(x * x, axis=-1, keepdims=True) + EPS)
    y = y * g_ref[...]
    o_ref[...] = (y * (1.0 + sc_ref[0]) + sh_ref[0]).astype(o_ref.dtype)


def _modnorm(x, g, shift, scale, *, rows_per_seq, tm):
    m, d = x.shape
    r = shift.shape[1]
    if r == 1:
        assert rows_per_seq % tm == 0
        smap = lambda i: ((i * tm) // rows_per_seq, 0, 0)
    else:
        assert r == tm
        smap = lambda i: (i, 0, 0)
    return pl.pallas_call(
        _modnorm_kernel,
        grid=(m // tm,),
        in_specs=[pl.BlockSpec((tm, d), lambda i: (i, 0)),
                  pl.BlockSpec((1, d), lambda i: (0, 0)),
                  pl.BlockSpec((1, r, d), smap),
                  pl.BlockSpec((1, r, d), smap)],
        out_specs=pl.BlockSpec((tm, d), lambda i: (i, 0)),
        out_shape=jax.ShapeDtypeStruct((m, d), BF16),
        compiler_params=_params("parallel"),
    )(x, g, shift, scale)


def _rmsnorm_kernel(x_ref, g_ref, o_ref):
    x = x_ref[...]
    y = x * lax.rsqrt(jnp.mean(x * x, axis=-1, keepdims=True) + EPS)
    o_ref[...] = y * g_ref[...]


def _rmsnorm(x, g, *, tm):
    m, d = x.shape
    return pl.pallas_call(
        _rmsnorm_kernel,
        grid=(m // tm,),
        in_specs=[pl.BlockSpec((tm, d), lambda i: (i, 0)),
                  pl.BlockSpec((1, d), lambda i: (0, 0))],
        out_specs=pl.BlockSpec((tm, d), lambda i: (i, 0)),
        out_shape=jax.ShapeDtypeStruct((m, d), F32),
        compiler_params=_params("parallel"),
    )(x, g)


def _log_sigmoid_pair(z):
    t = jnp.log1p(jnp.exp(-jnp.abs(z)))
    return jnp.minimum(z, 0.0) - t, jnp.minimum(-z, 0.0) - t


def _tri_stack():
    j = lax.broadcasted_iota(jnp.int32, (2 * CHUNK, CHUNK), 0) & (CHUNK - 1)
    s = lax.broadcasted_iota(jnp.int32, (2 * CHUNK, CHUNK), 1)
    return jnp.where(j > s, 1.0, 0.0).astype(BF16)


def _sb_block(z, visible, carry, tri2):
    ls, lk = _log_sigmoid_pair(z)
    if visible is not None:
        lk = jnp.where(visible, lk, 0.0)
    hi = lk.astype(BF16)
    lo = (lk - hi.astype(F32)).astype(BF16)
    later = carry + jnp.dot(jnp.concatenate([hi, lo], axis=1), tri2, preferred_element_type=F32)
    w = jnp.exp(ls + later)
    if visible is not None:
        w = jnp.where(visible, w, 0.0)
    return w, carry + jnp.sum(lk, axis=1, keepdims=True)


def _attn_prompt_kernel(bias_ref, q_ref, k_ref, v_ref, o_ref):
    h = pl.program_id(1)
    qi = pl.program_id(2)
    bias = bias_ref[h]
    q = (q_ref[0] * (SB_HEAD_DIM ** -0.5)).astype(BF16)
    tri2 = _tri_stack()
    t_idx = lax.broadcasted_iota(jnp.int32, (CHUNK, CHUNK), 0)
    s_idx = lax.broadcasted_iota(jnp.int32, (CHUNK, CHUNK), 1)

    def body(it, state):
        carry, acc = state
        kb = qi - it
        start = pl.multiple_of(kb * CHUNK, CHUNK)
        kblk = k_ref[0, pl.ds(start, CHUNK), :]
        vblk = v_ref[0, pl.ds(start, CHUNK), :]
        z = _bdot_nt(q, kblk) + bias
        visible = (s_idx - t_idx) < (qi - kb) * CHUNK
        w, carry = _sb_block(z, visible, carry, tri2)
        return carry, acc + _bdot(w, vblk)

    init = (jnp.zeros((CHUNK, 1), F32), jnp.zeros((CHUNK, SB_HEAD_DIM), F32))
    _, acc = lax.fori_loop(0, qi + 1, body, init)
    o_ref[0] = acc.astype(o_ref.dtype)


def _attn_prompt(proj, sb_bias):
    b, t, _ = proj.shape
    nq = t // CHUNK
    return pl.pallas_call(
        _attn_prompt_kernel,
        grid_spec=pltpu.PrefetchScalarGridSpec(
            num_scalar_prefetch=0,
            grid=(b, SB_HEADS, nq),
            in_specs=[pl.BlockSpec(memory_space=pltpu.SMEM),
                      pl.BlockSpec((1, CHUNK, SB_HEAD_DIM), lambda bi, h, qi: (bi, qi, h)),
                      pl.BlockSpec((1, t, SB_HEAD_DIM), lambda bi, h, qi: (bi, 0, SB_HEADS + h)),
                      pl.BlockSpec((1, t, SB_HEAD_DIM), lambda bi, h, qi: (bi, 0, 2 * SB_HEADS + h))],
            out_specs=pl.BlockSpec((1, CHUNK, SB_HEAD_DIM), lambda bi, h, qi: (bi, qi, h)),
        ),
        out_shape=jax.ShapeDtypeStruct((b, t, D_ATTN), BF16),
        compiler_params=_params("parallel", "parallel", "arbitrary"),
    )(sb_bias, proj, proj, proj)


def _attn_sample_kernel(pt_ref, bias_ref, q_ref, kn_ref, vn_ref, kc_ref, vc_ref, o_ref, acc_ref, carry_ref, *, tq):
    s = pl.program_id(1)
    rows = SB_HEADS * tq
    tri2 = _tri_stack()
    scale = SB_HEAD_DIM ** -0.5

    def head_cols(h):
        return slice(h * SB_HEAD_DIM, (h + 1) * SB_HEAD_DIM)

    def process(k_of, v_of, visible):
        z = jnp.concatenate(
            [_bdot_nt(q_ref[:, head_cols(h)] * scale, k_of(h)) + bias_ref[h] for h in range(SB_HEADS)], axis=0)
        w, carry = _sb_block(z, visible, carry_ref[...], tri2)
        carry_ref[...] = carry
        for h in range(SB_HEADS):
            acc_ref[h * tq:(h + 1) * tq, :] += _bdot(w[h * tq:(h + 1) * tq, :], v_of(h))

    @pl.when(s == 0)
    def _():
        carry_ref[...] = jnp.zeros_like(carry_ref)
        acc_ref[...] = jnp.zeros_like(acc_ref)
        pad = jnp.zeros((CHUNK - tq, SB_HEAD_DIM), F32)
        r_idx = lax.broadcasted_iota(jnp.int32, (rows, CHUNK), 0) & (tq - 1)
        s_idx = lax.broadcasted_iota(jnp.int32, (rows, CHUNK), 1)
        process(lambda h: jnp.concatenate([kn_ref[:, head_cols(h)], pad], axis=0),
                lambda h: jnp.concatenate([vn_ref[:, head_cols(h)], pad], axis=0),
                s_idx < r_idx)

    process(lambda h: kc_ref[0, :, head_cols(h)], lambda h: vc_ref[0, :, head_cols(h)], None)

    @pl.when(s == pl.num_programs(1) - 1)
    def _():
        for h in range(SB_HEADS):
            o_ref[:, head_cols(h)] = acc_ref[h * tq:(h + 1) * tq, :].astype(o_ref.dtype)


def _attn_sample(proj, cache_k, cache_v, page_table, sb_bias, *, tq):
    bs, n_pages = page_table.shape
    page = cache_k.shape[1]
    assert page == CHUNK and tq == SUBLANES
    nq = 1
    kv_map = lambda b, s, pt: (pt[b, n_pages - 1 - s], 0, 0)
    return pl.pallas_call(
        functools.partial(_attn_sample_kernel, tq=tq),
        grid_spec=pltpu.PrefetchScalarGridSpec(
            num_scalar_prefetch=1,
            grid=(bs, n_pages),
            in_specs=[pl.BlockSpec(memory_space=pltpu.SMEM),
                      pl.BlockSpec((tq, D_ATTN), lambda b, s, pt: (b, 0)),
                      pl.BlockSpec((tq, D_ATTN), lambda b, s, pt: (b, nq)),
                      pl.BlockSpec((tq, D_ATTN), lambda b, s, pt: (b, 2 * nq)),
                      pl.BlockSpec((1, page, D_ATTN), kv_map),
                      pl.BlockSpec((1, page, D_ATTN), kv_map)],
            out_specs=pl.BlockSpec((tq, D_ATTN), lambda b, s, pt: (b, 0)),
            scratch_shapes=[pltpu.VMEM((SB_HEADS * tq, SB_HEAD_DIM), F32),
                            pltpu.VMEM((SB_HEADS * tq, 1), F32)],
        ),
        out_shape=jax.ShapeDtypeStruct((bs * tq, D_ATTN), F32),
        compiler_params=_params("parallel", "arbitrary"),
    )(page_table, sb_bias, proj, proj, proj, cache_k, cache_v)


def _conv_kernel(cur_ref, prev_ref, buf_ref, w_ref, b_ref, o_ref, xp_ref):
    c = pl.program_id(1)
    prev = jnp.where(c == 0, buf_ref[0], prev_ref[0])
    xp_ref[0:SUBLANES, :] = prev
    xp_ref[SUBLANES:, :] = cur_ref[0]
    out = b_ref[...] + xp_ref[pl.ds(SUBLANES - CONV_WIDTH + 1, CHUNK), :] * w_ref[0:1, :]
    for i in range(1, CONV_WIDTH):
        out = out + xp_ref[pl.ds(SUBLANES - CONV_WIDTH + 1 + i, CHUNK), :] * w_ref[i:i + 1, :]
    o_ref[0] = _silu(out)


def _conv(proj, buf8, conv_w, conv_b, *, col0, tc):
    b, t, _ = proj.shape
    ch = conv_w.shape[1]
    c0 = col0 // tc
    per = CHUNK // SUBLANES
    return pl.pallas_call(
        _conv_kernel,
        grid=(b, t // CHUNK, ch // tc),
        in_specs=[pl.BlockSpec((1, CHUNK, tc), lambda bi, c, j: (bi, c, c0 + j)),
                  pl.BlockSpec((1, SUBLANES, tc), lambda bi, c, j: (bi, jnp.maximum(c * per - 1, 0), c0 + j)),
                  pl.BlockSpec((1, SUBLANES, tc), lambda bi, c, j: (bi, 0, j)),
                  pl.BlockSpec((CONV_WIDTH, tc), lambda bi, c, j: (0, j)),
                  pl.BlockSpec((1, tc), lambda bi, c, j: (0, j))],
        out_specs=pl.BlockSpec((1, CHUNK, tc), lambda bi, c, j: (bi, c, j)),
        out_shape=jax.ShapeDtypeStruct((b, t, ch), F32),
        scratch_shapes=[pltpu.VMEM((CHUNK + SUBLANES, tc), F32)],
        compiler_params=_params("parallel", "parallel", "parallel"),
    )(proj, proj, buf8, conv_w, conv_b)


def _ssd_kernel(xs_ref, bm_ref, cm_ref, z_ref, dt_ref, dtt_ref, h0_ref, dtb_row_ref, dtb_col_ref, alog_row_ref,
                alog_col_ref, dskip_ref, ng_ref, y_ref, hout_ref, ht_ref, *, t_valid):
    g = pl.program_id(1)
    c = pl.program_id(2)
    gw = xs_ref.shape[2]
    hpg = gw // SSM_HEAD_DIM
    f32dot = functools.partial(jnp.dot, preferred_element_type=F32)

    @pl.when(c == 0)
    def _():
        ht_ref[...] = h0_ref[0].T

    row = lax.broadcasted_iota(jnp.int32, (CHUNK, CHUNK), 0)
    col = lax.broadcasted_iota(jnp.int32, (CHUNK, CHUNK), 1)
    causal = row >= col
    tri_incl = jnp.where(causal, 1.0, 0.0).astype(BF16)
    tri_incl_t = jnp.where(col >= row, 1.0, 0.0).astype(BF16)

    e_row = lax.broadcasted_iota(jnp.int32, (LANES, gw), 0)
    e_col = lax.broadcasted_iota(jnp.int32, (LANES, gw), 1)
    expand = jnp.where(e_row == g * hpg + lax.shift_right_logical(e_col, HEAD_DIM_SHIFT), 1.0, 0.0).astype(BF16)

    t_col = c * CHUNK + lax.broadcasted_iota(jnp.int32, (CHUNK, gw), 0)
    dt_raw = _exact_dot(dt_ref[0], expand)
    dt = jnp.where(t_col < t_valid, _softplus(dt_raw + dtb_row_ref[...]), 0.0)
    a_row = -jnp.exp(alog_row_ref[...])
    hi, mid, lo = _split3(dt * a_row)
    a_cum = f32dot(tri_incl, hi) + f32dot(tri_incl, mid) + f32dot(tri_incl, lo)
    a_end = a_cum[CHUNK - 1:CHUNK, :]

    h_lo = pl.multiple_of(g * hpg, hpg)
    t_row = c * CHUNK + lax.broadcasted_iota(jnp.int32, (hpg, CHUNK), 1)
    dt_r = jnp.where(t_row < t_valid,
                     _softplus(dtt_ref[0, pl.ds(h_lo, hpg), :] + dtb_col_ref[pl.ds(h_lo, hpg), :]), 0.0)
    hi, mid, lo = _split3(dt_r * (-jnp.exp(alog_col_ref[pl.ds(h_lo, hpg), :])))
    a_cum_r = f32dot(hi, tri_incl_t) + f32dot(mid, tri_incl_t) + f32dot(lo, tri_incl_t)

    xs = xs_ref[0]
    bm = bm_ref[0]
    cm = cm_ref[0].astype(BF16)
    xdt = xs * dt
    xdt_b = xdt.astype(BF16)
    cb = _bdot_nt(cm, bm)

    ht = ht_ref[...]
    y = f32dot(cm, ht.astype(BF16)) * jnp.exp(a_cum)
    lane = lax.broadcasted_iota(jnp.int32, (CHUNK, LANES), 1)
    first_head = lane < SSM_HEAD_DIM
    y_diag = []
    for p in range(hpg // 2):
        cols = slice(p * LANES, (p + 1) * LANES)
        scores = []
        for j in (2 * p, 2 * p + 1):
            a_l = jnp.broadcast_to(a_cum[:, j * SSM_HEAD_DIM:j * SSM_HEAD_DIM + 1], (CHUNK, CHUNK))
            seg = a_l - a_cum_r[j:j + 1, :]
            scores.append((cb * jnp.exp(jnp.where(causal, seg, -jnp.inf))).astype(BF16))
        xp = xdt_b[:, cols]
        rhs = jnp.concatenate([jnp.where(first_head, xp, jnp.zeros_like(xp)),
                               jnp.where(first_head, jnp.zeros_like(xp), xp)], axis=0)
        y_diag.append(f32dot(jnp.concatenate(scores, axis=1), rhs))
    y = y + jnp.concatenate(y_diag, axis=1)

    xd = (xdt * jnp.exp(a_end - a_cum)).astype(BF16)
    ht_new = ht * jnp.exp(a_end) + f32dot(bm.T.astype(BF16), xd)
    ht_ref[...] = ht_new

    @pl.when(c == pl.num_programs(2) - 1)
    def _():
        hout_ref[0] = ht_new.T

    y = y + xs * dskip_ref[...]
    y = y * _silu(z_ref[0])
    y = y * lax.rsqrt(jnp.mean(y * y, axis=-1, keepdims=True) + EPS)
    y_ref[0] = (y * ng_ref[...]).astype(y_ref.dtype)


def _ssd(xc, proj, z_col0, dt, dtt, h0, dtb_row, dtb_col, alog_row, alog_col, dskip, ng, *, t_valid):
    b, t, cc = xc.shape
    d_ssm = cc - 2 * SSM_GROUPS * SSM_STATE
    gw = d_ssm // SSM_GROUPS
    nb0 = d_ssm // SSM_STATE
    z0 = z_col0 // gw
    gmap = lambda bi, g, c: (0, g)
    return pl.pallas_call(
        functools.partial(_ssd_kernel, t_valid=t_valid),
        grid=(b, SSM_GROUPS, t // CHUNK),
        in_specs=[pl.BlockSpec((1, CHUNK, gw), lambda bi, g, c: (bi, c, g)),
                  pl.BlockSpec((1, CHUNK, SSM_STATE), lambda bi, g, c: (bi, c, nb0 + g)),
                  pl.BlockSpec((1, CHUNK, SSM_STATE), lambda bi, g, c: (bi, c, nb0 + SSM_GROUPS + g)),
                  pl.BlockSpec((1, CHUNK, gw), lambda bi, g, c: (bi, c, z0 + g)),
                  pl.BlockSpec((1, CHUNK, LANES), lambda bi, g, c: (bi, c, 0)),
                  pl.BlockSpec((1, LANES, CHUNK), lambda bi, g, c: (bi, 0, c)),
                  pl.BlockSpec((1, gw, SSM_STATE), lambda bi, g, c: (bi, g, 0)),
                  pl.BlockSpec((1, gw), gmap),
                  pl.BlockSpec((LANES, 1), lambda bi, g, c: (0, 0)),
                  pl.BlockSpec((1, gw), gmap),
                  pl.BlockSpec((LANES, 1), lambda bi, g, c: (0, 0)),
                  pl.BlockSpec((1, gw), gmap),
                  pl.BlockSpec((1, gw), gmap)],
        out_specs=[pl.BlockSpec((1, CHUNK, gw), lambda bi, g, c: (bi, c, g)),
                   pl.BlockSpec((1, gw, SSM_STATE), lambda bi, g, c: (bi, g, 0))],
        out_shape=[jax.ShapeDtypeStruct((b, t, d_ssm), BF16),
                   jax.ShapeDtypeStruct((b, d_ssm, SSM_STATE), F32)],
        scratch_shapes=[pltpu.VMEM((SSM_STATE, gw), F32)],
        compiler_params=_params("parallel", "parallel", "arbitrary"),
    )(xc, xc, xc, proj, dt, dtt, h0, dtb_row, dtb_col, alog_row, alog_col, dskip, ng)


def _pad_heads(v):
    return jnp.pad(v, (0, LANES - v.shape[0]))


def _layer(x, mod, attn_fn, h0, conv_buf, lw, *, rows_per_seq, tm, tn_ffn):
    m, d = x.shape
    nseq = m // rows_per_seq
    (norm_g, w_ffn1_in, w_ffn1_out, w_in, w_dt, conv_w, conv_b, dt_bias, a_log, d_skip, ssm_norm_g, w_gate,
     b_gate, w_branch_attn, w_branch_ssm, w_out, w_ffn2_in, w_ffn2_out) = lw
    per_seq = rows_per_seq % tm == 0
    tnorm = 256 if per_seq else m
    tm_out = min(tm, 512)
    d_ssm = w_branch_ssm.shape[0]
    n_main = 3 * D_ATTN + d_ssm + conv_w.shape[1]
    rep = lambda v: jnp.repeat(v, SSM_HEAD_DIM)[None, :]

    def mvec(idx):
        v = mod[:, idx:idx + 1, :]
        return v if per_seq else jnp.repeat(v[:, 0, :], rows_per_seq, axis=0)[None]

    def modnorm(x, k):
        return _modnorm(x, norm_g[k:k + 1], mvec(3 * k), mvec(3 * k + 1), rows_per_seq=rows_per_seq, tm=tnorm)

    def ffn(x, k, w_a, w_b):
        hdn = _mm_swiglu(modnorm(x, k), w_a, tm=tm, tn=tn_ffn)
        return _mm_resid(hdn, w_b, x, mvec(3 * k + 2), scale=0.5, tm=tm_out, tn=tn_ffn,
                         rows_per_gate=rows_per_seq)

    x = ffn(x, 0, w_ffn1_in, w_ffn1_out)

    n = modnorm(x, 1)
    zero_bias = jnp.zeros((1, w_in.shape[1]), F32)
    proj = _mm_bias(n, w_in, zero_bias, col0=0, n=n_main, tm=tm, tn=512, out_dtype=F32)
    dt_raw = _mm_bias(n, w_dt, zero_bias[:, :LANES], col0=0, n=LANES, tm=tm, tn=LANES, out_dtype=F32)
    gates = _mm_sigmoid(n, w_gate, b_gate[None, :], tm=tm, tn=512)

    o_attn = attn_fn(proj)

    t_pad = -(-rows_per_seq // CHUNK) * CHUNK
    proj3 = proj.reshape(nseq, rows_per_seq, n_main)
    dt3 = dt_raw.reshape(nseq, rows_per_seq, LANES)
    if t_pad != rows_per_seq:
        padt = ((0, 0), (0, t_pad - rows_per_seq), (0, 0))
        proj3, dt3 = jnp.pad(proj3, padt), jnp.pad(dt3, padt)
    buf8 = jnp.pad(conv_buf, ((0, 0), (SUBLANES - conv_buf.shape[1], 0), (0, 0)))
    xc = _conv(proj3, buf8, conv_w, conv_b[None, :], col0=3 * D_ATTN + d_ssm, tc=512)
    y_ssm, h_new = _ssd(xc, proj3, 3 * D_ATTN, dt3, jnp.swapaxes(dt3, 1, 2), h0,
                        rep(dt_bias), _pad_heads(dt_bias)[:, None], rep(a_log), _pad_heads(a_log)[:, None],
                        rep(d_skip), ssm_norm_g[None, :], t_valid=rows_per_seq)
    o_ssm = y_ssm[:, :rows_per_seq].reshape(m, d_ssm)

    merged = _mm_merge(o_attn, w_branch_attn, o_ssm, w_branch_ssm, gates, tm=tm, tn=256)
    x = _mm_resid(merged, w_out, x, mvec(5), scale=1.0, tm=tm, tn=512, rows_per_gate=rows_per_seq)

    x = ffn(x, 2, w_ffn2_in, w_ffn2_out)
    return x, proj3, h_new


def kernel(x_prompt, x_sample, c_prompt, c_sample, cache_k, cache_v, page_table, state_ssm, state_conv, norm_g,
           w_mod, b_mod, w_ffn1_in, w_ffn1_out, w_in, sb_bias, conv_w, conv_b, dt_bias, a_log, d_skip, ssm_norm_g,
           w_gate, b_gate, w_branch_attn, w_branch_ssm, w_out, w_ffn2_in, w_ffn2_out, final_norm_g):
    bp, tp, d = x_prompt.shape
    bs, ts, _ = x_sample.shape
    depth = norm_g.shape[0]
    n_pool, page = cache_k.shape[1], cache_k.shape[2]
    d_ssm = w_branch_ssm.shape[1]
    n_heads = dt_bias.shape[1]
    conv_ch = conv_w.shape[2]
    n_main = 3 * D_ATTN + d_ssm + conv_ch

    xp = x_prompt.reshape(bp * tp, d)
    xs = x_sample.reshape(bs * ts, d)
    c_all = jnp.concatenate([c_prompt, c_sample], axis=0)
    c_rows = -(-c_all.shape[0] // SUBLANES) * SUBLANES
    c_all = jnp.pad(c_all, ((0, c_rows - c_all.shape[0]), (0, 0)))
    h0_prompt = jnp.zeros((bp, d_ssm, SSM_STATE), F32)
    conv0_prompt = jnp.zeros((bp, CONV_WIDTH - 1, conv_ch), F32)

    outs = {k: [] for k in ("kp", "vp", "hp", "cp", "ks", "vs", "hs", "cs")}
    for l in range(depth):
        mod = _mm_bias(c_all, w_mod[l], b_mod[l][None, :], col0=0, n=N_MOD * d, tm=c_rows, tn=512,
                       out_dtype=F32, pre_silu=True).reshape(c_rows, N_MOD, d)
        w_dt = jnp.pad(w_in[l][:, n_main:], ((0, 0), (0, LANES - n_heads)))
        lw = (norm_g[l], w_ffn1_in[l], w_ffn1_out[l], w_in[l], w_dt, conv_w[l], conv_b[l], dt_bias[l], a_log[l],
              d_skip[l], ssm_norm_g[l], w_gate[l], b_gate[l], w_branch_attn[l], w_branch_ssm[l], w_out[l],
              w_ffn2_in[l], w_ffn2_out[l])

        attn_p = lambda proj: _attn_prompt(proj.reshape(bp, tp, n_main), sb_bias[l]).reshape(bp * tp, D_ATTN)
        xp, proj_p, h_p = _layer(xp, mod[:bp], attn_p, h0_prompt, conv0_prompt, lw,
                                 rows_per_seq=tp, tm=min(tp, 1024), tn_ffn=256)
        outs["kp"].append(proj_p[:, :, D_ATTN:2 * D_ATTN].reshape(bp, tp, SB_HEADS, SB_HEAD_DIM))
        outs["vp"].append(proj_p[:, :, 2 * D_ATTN:3 * D_ATTN].reshape(bp, tp, SB_HEADS, SB_HEAD_DIM))
        outs["hp"].append(h_p.reshape(bp, n_heads, SSM_HEAD_DIM, SSM_STATE))
        outs["cp"].append(proj_p[:, tp - (CONV_WIDTH - 1):, 3 * D_ATTN + d_ssm:])

        ck = cache_k[l].reshape(n_pool, page, D_ATTN)
        cv = cache_v[l].reshape(n_pool, page, D_ATTN)
        attn_s = lambda proj: _attn_sample(proj, ck, cv, page_table, sb_bias[l], tq=ts)
        xs, proj_s, h_s = _layer(xs, mod[bp:bp + bs], attn_s, state_ssm[l].reshape(bs, d_ssm, SSM_STATE),
                                 state_conv[l], lw, rows_per_seq=ts, tm=bs * ts, tn_ffn=256)
        outs["ks"].append(proj_s[:, :ts, D_ATTN:2 * D_ATTN].reshape(bs, ts, SB_HEADS, SB_HEAD_DIM))
        outs["vs"].append(proj_s[:, :ts, 2 * D_ATTN:3 * D_ATTN].reshape(bs, ts, SB_HEADS, SB_HEAD_DIM))
        outs["hs"].append(h_s.reshape(bs, n_heads, SSM_HEAD_DIM, SSM_STATE))
        conv_in = jnp.concatenate([state_conv[l], proj_s[:, :ts, 3 * D_ATTN + d_ssm:]], axis=1)
        outs["cs"].append(conv_in[:, -(CONV_WIDTH - 1):])

    y_prompt = _rmsnorm(xp, final_norm_g[None, :], tm=256).reshape(bp, tp, d)
    y_sample = _rmsnorm(xs, final_norm_g[None, :], tm=bs * ts).reshape(bs, ts, d)
    st = lambda k: jnp.stack(outs[k])
    return (y_prompt, y_sample, st("kp"), st("vp"), st("hp"), st("cp"), st("ks"), st("vs"), st("hs"), st("cs"))
```

```python
import functools

import jax
import jax.numpy as jnp
from jax import lax
from jax.experimental import pallas as pl
from jax.experimental.pallas import tpu as pltpu

F32 = jnp.float32
BF16 = jnp.bfloat16

SB_HEADS = 16
SB_HEAD_DIM = 128
D_ATTN = SB_HEADS * SB_HEAD_DIM
SSM_HEAD_DIM = 64
HEAD_DIM_SHIFT = 6
SSM_GROUPS = 8
SSM_STATE = 128
CONV_WIDTH = 4
N_MOD = 9
EPS = 1e-6
CHUNK = 128
LANES = 128
SUBLANES = 8
VMEM_LIMIT = 56 * 1024 * 1024

ATTN_BLOCK = 256
ATTN_HEADS_PER_STEP = 4
PAGES_PER_STEP = 4
CONV_ROWS = 512


def _params(*sem):
    return pltpu.CompilerParams(dimension_semantics=sem, vmem_limit_bytes=VMEM_LIMIT)


def _wspec(block, imap, layer):
    return pl.BlockSpec((None,) + block, lambda *a: (layer,) + imap(*a))


def _sigmoid(x):
    return 1.0 / (1.0 + jnp.exp(-x))


def _silu(x):
    return x * _sigmoid(x)


def _softplus(x):
    return jnp.maximum(x, 0.0) + jnp.log1p(jnp.exp(-jnp.abs(x)))


def _bdot(a, b):
    return jnp.dot(a.astype(BF16), b.astype(BF16), preferred_element_type=F32)


def _bdot_nt(a, b):
    return lax.dot_general(a.astype(BF16), b.astype(BF16), (((1,), (1,)), ((), ())),
                           preferred_element_type=F32)


def _split3(x):
    hi = x.astype(BF16)
    r1 = x - hi.astype(F32)
    mid = r1.astype(BF16)
    lo = (r1 - mid.astype(F32)).astype(BF16)
    return hi, mid, lo


def _exact_dot(x, m01):
    hi, mid, lo = _split3(x)
    d = functools.partial(jnp.dot, preferred_element_type=F32)
    return d(hi, m01) + d(mid, m01) + d(lo, m01)


def _mm_bias_kernel(x_ref, w_ref, b_ref, *o_refs, pre_silu, out_scale):
    x = x_ref[...]
    if pre_silu:
        x = _silu(x.astype(F32))
    acc = _bdot(x, w_ref[...]) + b_ref[...]
    if out_scale != 1.0:
        acc = acc * out_scale
    for o_ref in o_refs:
        o_ref[...] = acc.astype(o_ref.dtype)


def _mm_bias(x, w, b, layer, *, col0, n, tm, tn, out_dtypes, pre_silu=False, out_scale=1.0, name):
    m, k = x.shape
    c0 = col0 // tn
    outs = pl.pallas_call(
        functools.partial(_mm_bias_kernel, pre_silu=pre_silu, out_scale=out_scale),
        grid=(m // tm, n // tn),
        in_specs=[pl.BlockSpec((tm, k), lambda i, j: (i, 0)),
                  _wspec((k, tn), lambda i, j: (0, c0 + j), layer),
                  pl.BlockSpec((1, tn), lambda i, j: (0, c0 + j))],
        out_specs=[pl.BlockSpec((tm, tn), lambda i, j: (i, j)) for _ in out_dtypes],
        out_shape=[jax.ShapeDtypeStruct((m, n), dt) for dt in out_dtypes],
        compiler_params=_params("parallel", "arbitrary"),
        name=name,
    )(x, w, b)
    return outs[0] if len(outs) == 1 else outs


def _mm_sigmoid_kernel(x_ref, w_ref, b_ref, o_ref):
    acc = _bdot(x_ref[...], w_ref[...])
    o_ref[...] = _sigmoid(acc + b_ref[...]).astype(o_ref.dtype)


def _mm_sigmoid(x, w, b, layer, *, tm, tn, name):
    m, k = x.shape
    n = w.shape[2]
    return pl.pallas_call(
        _mm_sigmoid_kernel,
        grid=(m // tm, n // tn),
        in_specs=[pl.BlockSpec((tm, k), lambda i, j: (i, 0)),
                  _wspec((k, tn), lambda i, j: (0, j), layer),
                  pl.BlockSpec((1, tn), lambda i, j: (0, j))],
        out_specs=pl.BlockSpec((tm, tn), lambda i, j: (i, j)),
        out_shape=jax.ShapeDtypeStruct((m, n), BF16),
        compiler_params=_params("parallel", "arbitrary"),
        name=name,
    )(x, w, b)


def _mm_swiglu_kernel(x_ref, wg_ref, wu_ref, o_ref):
    x = x_ref[...]
    gate = _bdot(x, wg_ref[...])
    up = _bdot(x, wu_ref[...])
    o_ref[...] = (_silu(gate) * up).astype(o_ref.dtype)


def _mm_swiglu(x, w, layer, *, tm, tn, name):
    m, k = x.shape
    f = w.shape[2] // 2
    nb = f // tn
    return pl.pallas_call(
        _mm_swiglu_kernel,
        grid=(m // tm, nb),
        in_specs=[pl.BlockSpec((tm, k), lambda i, j: (i, 0)),
                  _wspec((k, tn), lambda i, j: (0, j), layer),
                  _wspec((k, tn), lambda i, j: (0, nb + j), layer)],
        out_specs=pl.BlockSpec((tm, tn), lambda i, j: (i, j)),
        out_shape=jax.ShapeDtypeStruct((m, f), BF16),
        compiler_params=_params("parallel", "arbitrary"),
        name=name,
    )(x, w, w)


def _mm_resid_kernel(x_ref, w_ref, r_ref, g_ref, o_ref, *, scale):
    acc = _bdot(x_ref[...], w_ref[...])
    o_ref[...] = r_ref[...] + (scale * g_ref[0]) * acc


def _mm_resid(x, w, res, gate, layer, *, scale, tm, tn, rows_per_gate, name):
    m, k = x.shape
    n = w.shape[2]
    r = gate.shape[1]
    if r == 1:
        assert rows_per_gate % tm == 0
        gmap = lambda i, j: ((i * tm) // rows_per_gate, 0, j)
    else:
        assert r == tm
        gmap = lambda i, j: (i, 0, j)
    return pl.pallas_call(
        functools.partial(_mm_resid_kernel, scale=scale),
        grid=(m // tm, n // tn),
        in_specs=[pl.BlockSpec((tm, k), lambda i, j: (i, 0)),
                  _wspec((k, tn), lambda i, j: (0, j), layer),
                  pl.BlockSpec((tm, tn), lambda i, j: (i, j)),
                  pl.BlockSpec((1, r, tn), gmap)],
        out_specs=pl.BlockSpec((tm, tn), lambda i, j: (i, j)),
        out_shape=jax.ShapeDtypeStruct((m, n), F32),
        compiler_params=_params("parallel", "arbitrary"),
        name=name,
    )(x, w, res, gate)


def _mm_merge_kernel(oa_ref, wa_ref, os_ref, ws_ref, ga_ref, gs_ref, o_ref):
    a = _bdot(oa_ref[...], wa_ref[...])
    s = _bdot(os_ref[...], ws_ref[...])
    o_ref[...] = (ga_ref[...].astype(F32) * a + gs_ref[...].astype(F32) * s).astype(o_ref.dtype)


def _mm_merge(oa, wa, os_, ws, gates, layer, *, tm, tn, name):
    m = oa.shape[0]
    n = wa.shape[2]
    nb = n // tn
    return pl.pallas_call(
        _mm_merge_kernel,
        grid=(m // tm, nb),
        in_specs=[pl.BlockSpec((tm, oa.shape[1]), lambda i, j: (i, 0)),
                  _wspec((wa.shape[1], tn), lambda i, j: (0, j), layer),
                  pl.BlockSpec((tm, os_.shape[1]), lambda i, j: (i, 0)),
                  _wspec((ws.shape[1], tn), lambda i, j: (0, j), layer),
                  pl.BlockSpec((tm, tn), lambda i, j: (i, j)),
                  pl.BlockSpec((tm, tn), lambda i, j: (i, nb + j))],
        out_specs=pl.BlockSpec((tm, tn), lambda i, j: (i, j)),
        out_shape=jax.ShapeDtypeStruct((m, n), BF16),
        compiler_params=_params("parallel", "arbitrary"),
        name=name,
    )(oa, wa, os_, ws, gates, gates)


def _modnorm_kernel(x_ref, g_ref, sh_ref, sc_ref, o_ref):
    x = x_ref[...]
    y = x * lax.rsqrt(jnp.mean(x * x, axis=-1, keepdims=True) + EPS)
    y = y * g_ref[...]
    o_ref[...] = (y * (1.0 + sc_ref[0]) + sh_ref[0]).astype(o_ref.dtype)


def _modnorm(x, g, shift, scale, *, rows_per_seq, tm, name):
    m, d = x.shape
    r = shift.shape[1]
    if r == 1:
        assert rows_per_seq % tm == 0
        smap = lambda i: ((i * tm) // rows_per_seq, 0, 0)
    else:
        assert r == tm
        smap = lambda i: (i, 0, 0)
    return pl.pallas_call(
        _modnorm_kernel,
        grid=(m // tm,),
        in_specs=[pl.BlockSpec((tm, d), lambda i: (i, 0)),
                  pl.BlockSpec((1, d), lambda i: (0, 0)),
                  pl.BlockSpec((1, r, d), smap),
                  pl.BlockSpec((1, r, d), smap)],
        out_specs=pl.BlockSpec((tm, d), lambda i: (i, 0)),
        out_shape=jax.ShapeDtypeStruct((m, d), BF16),
        compiler_params=_params("parallel"),
        name=name,
    )(x, g, shift, scale)


def _rmsnorm_kernel(x_ref, g_ref, o_ref):
    x = x_ref[...]
    y = x * lax.rsqrt(jnp.mean(x * x, axis=-1, keepdims=True) + EPS)
    o_ref[...] = y * g_ref[...]


def _rmsnorm(x, g, *, tm, name):
    m, d = x.shape
    return pl.pallas_call(
        _rmsnorm_kernel,
        grid=(m // tm,),
        in_specs=[pl.BlockSpec((tm, d), lambda i: (i, 0)),
                  pl.BlockSpec((1, d), lambda i: (0, 0))],
        out_specs=pl.BlockSpec((tm, d), lambda i: (i, 0)),
        out_shape=jax.ShapeDtypeStruct((m, d), F32),
        compiler_params=_params("parallel"),
        name=name,
    )(x, g)


def _cumsum_rhs():
    j = lax.broadcasted_iota(jnp.int32, (2 * LANES, 2 * LANES), 0) & (LANES - 1)
    s = lax.broadcasted_iota(jnp.int32, (2 * LANES, 2 * LANES), 1)
    return jnp.where((j > s) | (s >= LANES), 1.0, 0.0).astype(BF16)


def _sb_weights(z_tiles, vis_tiles, carry, rhs):
    ls_t, part_t = [], []
    for z, vis in zip(z_tiles, vis_tiles):
        t = jnp.log(1.0 + jnp.exp(-jnp.abs(z)))
        ls_t.append(jnp.minimum(z, 0.0) - t)
        lk = -jnp.maximum(z, 0.0) - t
        if vis is not None:
            lk = jnp.where(vis, lk, 0.0)
        hi = lk.astype(BF16)
        lo = (lk - hi.astype(F32)).astype(BF16)
        part_t.append(jnp.dot(jnp.concatenate([hi, lo], axis=1), rhs, preferred_element_type=F32))
    w_t = [None] * len(z_tiles)
    for i in range(len(z_tiles) - 1, -1, -1):
        later = carry + part_t[i][:, :LANES]
        w = jnp.exp(ls_t[i] + later)
        if vis_tiles[i] is not None:
            w = jnp.where(vis_tiles[i], w, 0.0)
        w_t[i] = w
        carry = carry + part_t[i][:, LANES:]
    return w_t, carry


def _attn_prompt_kernel(bias_ref, q_ref, k_ref, v_ref, o_ref, acc_ref, carry_ref):
    hg = pl.program_id(1)
    qi = pl.program_id(2)
    nh = ATTN_HEADS_PER_STEP
    nt = ATTN_BLOCK // LANES
    rhs = _cumsum_rhs()
    row = lax.broadcasted_iota(jnp.int32, (ATTN_BLOCK, LANES), 0)
    col = lax.broadcasted_iota(jnp.int32, (ATTN_BLOCK, LANES), 1)
    diag_vis = [col + t * LANES < row for t in range(nt)]

    def hcols(h):
        return slice(h * SB_HEAD_DIM, (h + 1) * SB_HEAD_DIM)

    def block(kb, vis_tiles, first):
        start = pl.multiple_of(kb * ATTN_BLOCK, ATTN_BLOCK)
        for h in range(nh):
            bias = bias_ref[hg * nh + h]
            q = q_ref[0, :, hcols(h)]
            kblk = k_ref[0, pl.ds(start, ATTN_BLOCK), hcols(h)]
            vblk = v_ref[0, pl.ds(start, ATTN_BLOCK), hcols(h)]
            z = _bdot_nt(q, kblk) + bias
            carry = jnp.zeros((ATTN_BLOCK, LANES), F32) if first else carry_ref[h]
            w_t, carry = _sb_weights([z[:, t * LANES:(t + 1) * LANES] for t in range(nt)], vis_tiles, carry, rhs)
            carry_ref[h] = carry
            pv = jnp.dot(jnp.concatenate([w.astype(BF16) for w in w_t], axis=1), vblk, preferred_element_type=F32)
            acc_ref[h] = pv if first else acc_ref[h] + pv

    block(qi, diag_vis, True)

    def body(it, _):
        block(qi - 1 - it, [None] * nt, False)
        return 0

    lax.fori_loop(0, qi, body, 0)
    for h in range(nh):
        o_ref[0, :, hcols(h)] = acc_ref[h].astype(o_ref.dtype)


def _attn_prompt(q, k, v, sb_bias):
    b, t, _ = q.shape
    nq = t // ATTN_BLOCK
    nh = ATTN_HEADS_PER_STEP
    w = nh * SB_HEAD_DIM
    return pl.pallas_call(
        _attn_prompt_kernel,
        grid=(b, SB_HEADS // nh, nq),
        in_specs=[pl.BlockSpec(memory_space=pltpu.SMEM),
                  pl.BlockSpec((1, ATTN_BLOCK, w), lambda bi, h, qi: (bi, qi, h)),
                  pl.BlockSpec((1, t, w), lambda bi, h, qi: (bi, 0, h)),
                  pl.BlockSpec((1, t, w), lambda bi, h, qi: (bi, 0, h))],
        out_specs=pl.BlockSpec((1, ATTN_BLOCK, w), lambda bi, h, qi: (bi, qi, h)),
        out_shape=jax.ShapeDtypeStruct((b, t, D_ATTN), BF16),
        scratch_shapes=[pltpu.VMEM((nh, ATTN_BLOCK, SB_HEAD_DIM), F32),
                        pltpu.VMEM((nh, ATTN_BLOCK, LANES), F32)],
        compiler_params=_params("parallel", "parallel", "arbitrary"),
        name="attn_prompt",
    )(sb_bias, q, k, v)


def _attn_sample_kernel(pt_ref, bias_ref, q_ref, kn_ref, vn_ref, *refs, tq):
    npg = PAGES_PER_STEP
    kc_refs, vc_refs = refs[:npg], refs[npg:2 * npg]
    o_ref, acc_ref, carry_ref = refs[2 * npg:]
    s = pl.program_id(1)
    rows = SB_HEADS * tq
    rhs = _cumsum_rhs()

    def hcols(h):
        return slice(h * SB_HEAD_DIM, (h + 1) * SB_HEAD_DIM)

    def process(k_tiles, v_tiles, vis):
        z_t = [jnp.concatenate([_bdot_nt(q_ref[:, hcols(h)], k_of(h)) + bias_ref[h] for h in range(SB_HEADS)], axis=0)
               for k_of in k_tiles]
        w_t, carry = _sb_weights(z_t, [vis] * len(z_t), carry_ref[...], rhs)
        carry_ref[...] = carry
        for h in range(SB_HEADS):
            w_h = jnp.concatenate([w[h * tq:(h + 1) * tq, :].astype(BF16) for w in w_t], axis=1)
            v_h = jnp.concatenate([v_of(h).astype(BF16) for v_of in v_tiles], axis=0)
            acc_ref[h * tq:(h + 1) * tq, :] += jnp.dot(w_h, v_h, preferred_element_type=F32)

    @pl.when(s == 0)
    def _():
        carry_ref[...] = jnp.zeros_like(carry_ref)
        acc_ref[...] = jnp.zeros_like(acc_ref)
        pad = jnp.zeros((CHUNK - tq, SB_HEAD_DIM), F32)
        r_idx = lax.broadcasted_iota(jnp.int32, (rows, CHUNK), 0) & (tq - 1)
        s_idx = lax.broadcasted_iota(jnp.int32, (rows, CHUNK), 1)
        process([lambda h: jnp.concatenate([kn_ref[:, hcols(h)], pad], axis=0)],
                [lambda h: jnp.concatenate([vn_ref[:, hcols(h)], pad], axis=0)],
                s_idx < r_idx)

    process([functools.partial(lambda r, h: r[:, hcols(h)], r) for r in reversed(kc_refs)],
            [functools.partial(lambda r, h: r[:, hcols(h)], r) for r in reversed(vc_refs)], None)

    @pl.when(s == pl.num_programs(1) - 1)
    def _():
        for h in range(SB_HEADS):
            o_ref[:, hcols(h)] = acc_ref[h * tq:(h + 1) * tq, :].astype(o_ref.dtype)


def _attn_sample(q, kn, vn, cache_k, cache_v, page_table, sb_bias, layer, *, tq):
    bs, n_pages = page_table.shape
    page = cache_k.shape[2]
    npg = PAGES_PER_STEP
    assert page == CHUNK and tq == SUBLANES and n_pages % npg == 0
    new_spec = pl.BlockSpec((tq, D_ATTN), lambda b, s, pt: (b, 0))

    def page_spec(i):
        return pl.BlockSpec((None, None, page, D_ATTN),
                            lambda b, s, pt: (layer, pt[b, n_pages - 1 - (s * npg + i)], 0, 0))

    return pl.pallas_call(
        functools.partial(_attn_sample_kernel, tq=tq),
        grid_spec=pltpu.PrefetchScalarGridSpec(
            num_scalar_prefetch=1,
            grid=(bs, n_pages // npg),
            in_specs=[pl.BlockSpec(memory_space=pltpu.SMEM), new_spec, new_spec, new_spec]
                     + [page_spec(i) for i in range(npg)] * 2,
            out_specs=pl.BlockSpec((tq, D_ATTN), lambda b, s, pt: (b, 0)),
            scratch_shapes=[pltpu.VMEM((SB_HEADS * tq, SB_HEAD_DIM), F32),
                            pltpu.VMEM((SB_HEADS * tq, LANES), F32)],
        ),
        out_shape=jax.ShapeDtypeStruct((bs * tq, D_ATTN), F32),
        compiler_params=_params("parallel", "arbitrary"),
        name="attn_sample",
    )(page_table, sb_bias, q, kn, vn, *([cache_k] * npg), *([cache_v] * npg))


def _conv_kernel(cur_ref, prev_ref, buf_ref, w_ref, b_ref, o_ref, xp_ref):
    c = pl.program_id(1)
    rows = cur_ref.shape[1]
    prev = jnp.where(c == 0, buf_ref[0], prev_ref[0])
    xp_ref[0:SUBLANES, :] = prev
    xp_ref[SUBLANES:, :] = cur_ref[0]
    out = b_ref[...] + xp_ref[pl.ds(SUBLANES - CONV_WIDTH + 1, rows), :] * w_ref[0:1, :]
    for i in range(1, CONV_WIDTH):
        out = out + xp_ref[pl.ds(SUBLANES - CONV_WIDTH + 1 + i, rows), :] * w_ref[i:i + 1, :]
    o_ref[0] = _silu(out)


def _conv(zx, buf8, conv_w, conv_b, *, col0, tc, name):
    b, t, _ = zx.shape
    ch = conv_w.shape[1]
    c0 = col0 // tc
    rows = min(t, CONV_ROWS)
    per = rows // SUBLANES
    return pl.pallas_call(
        _conv_kernel,
        grid=(b, t // rows, ch // tc),
        in_specs=[pl.BlockSpec((1, rows, tc), lambda bi, c, j: (bi, c, c0 + j)),
                  pl.BlockSpec((1, SUBLANES, tc), lambda bi, c, j: (bi, jnp.maximum(c * per - 1, 0), c0 + j)),
                  pl.BlockSpec((1, SUBLANES, tc), lambda bi, c, j: (bi, 0, j)),
                  pl.BlockSpec((CONV_WIDTH, tc), lambda bi, c, j: (0, j)),
                  pl.BlockSpec((1, tc), lambda bi, c, j: (0, j))],
        out_specs=pl.BlockSpec((1, rows, tc), lambda bi, c, j: (bi, c, j)),
        out_shape=jax.ShapeDtypeStruct((b, t, ch), F32),
        scratch_shapes=[pltpu.VMEM((rows + SUBLANES, tc), F32)],
        compiler_params=_params("parallel", "parallel", "parallel"),
        name=name,
    )(zx, zx, buf8, conv_w, conv_b)


def _ssd_kernel(xs_ref, bm_ref, cm_ref, z_ref, dt_ref, dtt_ref, h0_ref, dtb_row_ref, dtb_col_ref, alog_row_ref,
                alog_col_ref, dskip_ref, ng_ref, y_ref, hout_ref, ht_ref, *, t_valid):
    g = pl.program_id(1)
    c = pl.program_id(2)
    gw = xs_ref.shape[2]
    hpg = gw // SSM_HEAD_DIM
    f32dot = functools.partial(jnp.dot, preferred_element_type=F32)

    @pl.when(c == 0)
    def _():
        ht_ref[...] = h0_ref[0].T

    row = lax.broadcasted_iota(jnp.int32, (CHUNK, CHUNK), 0)
    col = lax.broadcasted_iota(jnp.int32, (CHUNK, CHUNK), 1)
    causal = row >= col
    tri_incl = jnp.where(causal, 1.0, 0.0).astype(BF16)
    tri_incl_t = jnp.where(col >= row, 1.0, 0.0).astype(BF16)

    e_row = lax.broadcasted_iota(jnp.int32, (LANES, gw), 0)
    e_col = lax.broadcasted_iota(jnp.int32, (LANES, gw), 1)
    expand = jnp.where(e_row == g * hpg + lax.shift_right_logical(e_col, HEAD_DIM_SHIFT), 1.0, 0.0).astype(BF16)

    t_col = c * CHUNK + lax.broadcasted_iota(jnp.int32, (CHUNK, gw), 0)
    dt_raw = _exact_dot(dt_ref[0], expand)
    dt = jnp.where(t_col < t_valid, _softplus(dt_raw + dtb_row_ref[...]), 0.0)
    a_row = -jnp.exp(alog_row_ref[...])
    hi, mid, lo = _split3(dt * a_row)
    a_cum = f32dot(tri_incl, hi) + f32dot(tri_incl, mid) + f32dot(tri_incl, lo)
    a_end = a_cum[CHUNK - 1:CHUNK, :]

    h_lo = pl.multiple_of(g * hpg, hpg)
    t_row = c * CHUNK + lax.broadcasted_iota(jnp.int32, (hpg, CHUNK), 1)
    dt_r = jnp.where(t_row < t_valid,
                     _softplus(dtt_ref[0, pl.ds(h_lo, hpg), :] + dtb_col_ref[pl.ds(h_lo, hpg), :]), 0.0)
    hi, mid, lo = _split3(dt_r * (-jnp.exp(alog_col_ref[pl.ds(h_lo, hpg), :])))
    a_cum_r = f32dot(hi, tri_incl_t) + f32dot(mid, tri_incl_t) + f32dot(lo, tri_incl_t)

    xs = xs_ref[0]
    bm = bm_ref[0]
    cm = cm_ref[0].astype(BF16)
    xdt = xs * dt
    xdt_b = xdt.astype(BF16)
    cb = _bdot_nt(cm, bm)

    ht = ht_ref[...]
    y = f32dot(cm, ht.astype(BF16)) * jnp.exp(a_cum)
    lane = lax.broadcasted_iota(jnp.int32, (CHUNK, LANES), 1)
    first_head = lane < SSM_HEAD_DIM
    y_diag = []
    for p in range(hpg // 2):
        cols = slice(p * LANES, (p + 1) * LANES)
        scores = []
        for j in (2 * p, 2 * p + 1):
            a_l = jnp.broadcast_to(a_cum[:, j * SSM_HEAD_DIM:j * SSM_HEAD_DIM + 1], (CHUNK, CHUNK))
            seg = a_l - a_cum_r[j:j + 1, :]
            scores.append((cb * jnp.exp(jnp.where(causal, seg, -jnp.inf))).astype(BF16))
        xp = xdt_b[:, cols]
        rhs = jnp.concatenate([jnp.where(first_head, xp, jnp.zeros_like(xp)),
                               jnp.where(first_head, jnp.zeros_like(xp), xp)], axis=0)
        y_diag.append(f32dot(jnp.concatenate(scores, axis=1), rhs))
    y = y + jnp.concatenate(y_diag, axis=1)

    xd = (xdt * jnp.exp(a_end - a_cum)).astype(BF16)
    ht_new = ht * jnp.exp(a_end) + f32dot(bm.T.astype(BF16), xd)
    ht_ref[...] = ht_new

    @pl.when(c == pl.num_programs(2) - 1)
    def _():
        hout_ref[0] = ht_new.T

    y = y + xs * dskip_ref[...]
    y = y * _silu(z_ref[0])
    y = y * lax.rsqrt(jnp.mean(y * y, axis=-1, keepdims=True) + EPS)
    y_ref[0] = (y * ng_ref[...]).astype(y_ref.dtype)


def _ssd(xc, zx, dt, dtt, h0, dtb_row, dtb_col, alog_row, alog_col, dskip, ng, *, t_valid, name):
    b, t, cc = xc.shape
    d_ssm = cc - 2 * SSM_GROUPS * SSM_STATE
    gw = d_ssm // SSM_GROUPS
    nb0 = d_ssm // SSM_STATE
    gmap = lambda bi, g, c: (0, g)
    return pl.pallas_call(
        functools.partial(_ssd_kernel, t_valid=t_valid),
        grid=(b, SSM_GROUPS, t // CHUNK),
        in_specs=[pl.BlockSpec((1, CHUNK, gw), lambda bi, g, c: (bi, c, g)),
                  pl.BlockSpec((1, CHUNK, SSM_STATE), lambda bi, g, c: (bi, c, nb0 + g)),
                  pl.BlockSpec((1, CHUNK, SSM_STATE), lambda bi, g, c: (bi, c, nb0 + SSM_GROUPS + g)),
                  pl.BlockSpec((1, CHUNK, gw), lambda bi, g, c: (bi, c, g)),
                  pl.BlockSpec((1, CHUNK, LANES), lambda bi, g, c: (bi, c, 0)),
                  pl.BlockSpec((1, LANES, CHUNK), lambda bi, g, c: (bi, 0, c)),
                  pl.BlockSpec((1, gw, SSM_STATE), lambda bi, g, c: (bi, g, 0)),
                  pl.BlockSpec((1, gw), gmap),
                  pl.BlockSpec((LANES, 1), lambda bi, g, c: (0, 0)),
                  pl.BlockSpec((1, gw), gmap),
                  pl.BlockSpec((LANES, 1), lambda bi, g, c: (0, 0)),
                  pl.BlockSpec((1, gw), gmap),
                  pl.BlockSpec((1, gw), gmap)],
        out_specs=[pl.BlockSpec((1, CHUNK, gw), lambda bi, g, c: (bi, c, g)),
                   pl.BlockSpec((1, gw, SSM_STATE), lambda bi, g, c: (bi, g, 0))],
        out_shape=[jax.ShapeDtypeStruct((b, t, d_ssm), BF16),
                   jax.ShapeDtypeStruct((b, d_ssm, SSM_STATE), F32)],
        scratch_shapes=[pltpu.VMEM((SSM_STATE, gw), F32)],
        compiler_params=_params("parallel", "parallel", "arbitrary"),
        name=name,
    )(xc, xc, xc, zx, dt, dtt, h0, dtb_row, dtb_col, alog_row, alog_col, dskip, ng)


def _pad_heads(v):
    return jnp.pad(v, (0, LANES - v.shape[0]))


def _layer(x, mod, attn_fn, h0, conv_buf, wts, small, layer, *, rows_per_seq, tm, tn_ffn, q_dtype, kv_dtypes, tag):
    m, d = x.shape
    nseq = m // rows_per_seq
    (w_ffn1_in, w_ffn1_out, w_in, w_gate, w_branch_attn, w_branch_ssm, w_out, w_ffn2_in, w_ffn2_out) = wts
    (norm_g, w_dt, conv_w, conv_b, dt_bias, a_log, d_skip, ssm_norm_g, b_gate) = small
    per_seq = rows_per_seq % tm == 0
    tnorm = 256 if per_seq else m
    tm_out = min(tm, 512)
    d_ssm = w_branch_ssm.shape[1]
    conv_ch = conv_w.shape[1]
    rep = lambda v: jnp.repeat(v, SSM_HEAD_DIM)[None, :]
    nm = lambda s: f"{s}_{tag}"

    def mvec(idx):
        v = mod[:, idx:idx + 1, :]
        return v if per_seq else jnp.repeat(v[:, 0, :], rows_per_seq, axis=0)[None]

    def modnorm(x, k):
        return _modnorm(x, norm_g[k:k + 1], mvec(3 * k), mvec(3 * k + 1), rows_per_seq=rows_per_seq, tm=tnorm,
                        name=nm(f"modnorm{k}"))

    def ffn(x, k, w_a, w_b):
        hdn = _mm_swiglu(modnorm(x, k), w_a, layer, tm=tm, tn=tn_ffn, name=nm(f"ffn{k}_in"))
        return _mm_resid(hdn, w_b, x, mvec(3 * k + 2), layer, scale=0.5, tm=tm_out, tn=tn_ffn,
                         rows_per_gate=rows_per_seq, name=nm(f"ffn{k}_out"))

    x = ffn(x, 0, w_ffn1_in, w_ffn1_out)

    n = modnorm(x, 1)
    zero_bias = jnp.zeros((1, w_in.shape[2]), F32)
    proj = functools.partial(_mm_bias, n, w_in, zero_bias, layer, tm=tm, tn=512)
    q = proj(col0=0, n=D_ATTN, out_dtypes=(q_dtype,), out_scale=SB_HEAD_DIM ** -0.5, name=nm("proj_q"))
    k = proj(col0=D_ATTN, n=D_ATTN, out_dtypes=kv_dtypes, name=nm("proj_k"))
    v = proj(col0=2 * D_ATTN, n=D_ATTN, out_dtypes=kv_dtypes, name=nm("proj_v"))
    zx = proj(col0=3 * D_ATTN, n=d_ssm + conv_ch, out_dtypes=(F32,), name=nm("proj_zx"))
    dt_raw = _mm_bias(n, w_dt, zero_bias[:, :LANES], 0, col0=0, n=LANES, tm=tm, tn=LANES, out_dtypes=(F32,),
                      name=nm("proj_dt"))
    gates = _mm_sigmoid(n, w_gate, b_gate[None, :], layer, tm=tm, tn=512, name=nm("gates"))

    o_attn = attn_fn(q, k, v)

    t_pad = -(-rows_per_seq // CHUNK) * CHUNK
    zx3 = zx.reshape(nseq, rows_per_seq, d_ssm + conv_ch)
    dt3 = dt_raw.reshape(nseq, rows_per_seq, LANES)
    if t_pad != rows_per_seq:
        padt = ((0, 0), (0, t_pad - rows_per_seq), (0, 0))
        zx3, dt3 = jnp.pad(zx3, padt), jnp.pad(dt3, padt)
    buf8 = jnp.pad(conv_buf, ((0, 0), (SUBLANES - conv_buf.shape[1], 0), (0, 0)))
    xc = _conv(zx3, buf8, conv_w, conv_b[None, :], col0=d_ssm, tc=1024, name=nm("conv"))
    y_ssm, h_new = _ssd(xc, zx3, dt3, jnp.swapaxes(dt3, 1, 2), h0,
                        rep(dt_bias), _pad_heads(dt_bias)[:, None], rep(a_log), _pad_heads(a_log)[:, None],
                        rep(d_skip), ssm_norm_g[None, :], t_valid=rows_per_seq, name=nm("ssd"))
    o_ssm = y_ssm[:, :rows_per_seq].reshape(m, d_ssm)

    merged = _mm_merge(o_attn, w_branch_attn, o_ssm, w_branch_ssm, gates, layer, tm=tm, tn=256, name=nm("merge"))
    x = _mm_resid(merged, w_out, x, mvec(5), layer, scale=1.0, tm=tm, tn=512, rows_per_gate=rows_per_seq,
                  name=nm("mix_out"))

    x = ffn(x, 2, w_ffn2_in, w_ffn2_out)
    return x, k, v, zx3, h_new


def kernel(x_prompt, x_sample, c_prompt, c_sample, cache_k, cache_v, page_table, state_ssm, state_conv, norm_g,
           w_mod, b_mod, w_ffn1_in, w_ffn1_out, w_in, sb_bias, conv_w, conv_b, dt_bias, a_log, d_skip, ssm_norm_g,
           w_gate, b_gate, w_branch_attn, w_branch_ssm, w_out, w_ffn2_in, w_ffn2_out, final_norm_g):
    bp, tp, d = x_prompt.shape
    bs, ts, _ = x_sample.shape
    depth = norm_g.shape[0]
    n_pool, page = cache_k.shape[1], cache_k.shape[2]
    d_ssm = w_branch_ssm.shape[1]
    n_heads = dt_bias.shape[1]
    conv_ch = conv_w.shape[2]
    n_main = 3 * D_ATTN + d_ssm + conv_ch

    xp = x_prompt.reshape(bp * tp, d)
    xs = x_sample.reshape(bs * ts, d)
    c_all = jnp.concatenate([c_prompt, c_sample], axis=0)
    c_rows = -(-c_all.shape[0] // SUBLANES) * SUBLANES
    c_all = jnp.pad(c_all, ((0, c_rows - c_all.shape[0]), (0, 0)))
    h0_prompt = jnp.zeros((bp, d_ssm, SSM_STATE), F32)
    conv0_prompt = jnp.zeros((bp, CONV_WIDTH - 1, conv_ch), F32)
    ck = cache_k.reshape(depth, n_pool, page, D_ATTN)
    cv = cache_v.reshape(depth, n_pool, page, D_ATTN)
    wts = (w_ffn1_in, w_ffn1_out, w_in, w_gate, w_branch_attn, w_branch_ssm, w_out, w_ffn2_in, w_ffn2_out)

    outs = {k: [] for k in ("kp", "vp", "hp", "cp", "ks", "vs", "hs", "cs")}
    for l in range(depth):
        mod = _mm_bias(c_all, w_mod, b_mod[l][None, :], l, col0=0, n=N_MOD * d, tm=c_rows, tn=512,
                       out_dtypes=(F32,), pre_silu=True, name=f"mod_l{l}").reshape(c_rows, N_MOD, d)
        w_dt = jnp.pad(w_in[l, :, n_main:], ((0, 0), (0, LANES - n_heads)))[None]
        small = (norm_g[l], w_dt, conv_w[l], conv_b[l], dt_bias[l], a_log[l], d_skip[l], ssm_norm_g[l], b_gate[l])

        def attn_p(q, k, v):
            r3 = lambda a: a.reshape(bp, tp, D_ATTN)
            return _attn_prompt(r3(q), r3(k[1]), r3(v[1]), sb_bias[l]).reshape(bp * tp, D_ATTN)

        xp, k_p, v_p, zx_p, h_p = _layer(xp, mod[:bp], attn_p, h0_prompt, conv0_prompt, wts, small, l,
                                         rows_per_seq=tp, tm=min(tp, 1024), tn_ffn=256, q_dtype=BF16,
                                         kv_dtypes=(F32, BF16), tag=f"p{l}")
        outs["kp"].append(k_p[0].reshape(bp, tp, SB_HEADS, SB_HEAD_DIM))
        outs["vp"].append(v_p[0].reshape(bp, tp, SB_HEADS, SB_HEAD_DIM))
        outs["hp"].append(h_p.reshape(bp, n_heads, SSM_HEAD_DIM, SSM_STATE))
        outs["cp"].append(zx_p[:, tp - (CONV_WIDTH - 1):, d_ssm:])

        def attn_s(q, k, v):
            return _attn_sample(q, k, v, ck, cv, page_table, sb_bias[l], l, tq=ts)

        xs, k_s, v_s, zx_s, h_s = _layer(xs, mod[bp:bp + bs], attn_s, state_ssm[l].reshape(bs, d_ssm, SSM_STATE),
                                         state_conv[l], wts, small, l, rows_per_seq=ts, tm=bs * ts, tn_ffn=256,
                                         q_dtype=F32, kv_dtypes=(F32,), tag=f"s{l}")
        outs["ks"].append(k_s.reshape(bs, ts, SB_HEADS, SB_HEAD_DIM))
        outs["vs"].append(v_s.reshape(bs, ts, SB_HEADS, SB_HEAD_DIM))
        outs["hs"].append(h_s.reshape(bs, n_heads, SSM_HEAD_DIM, SSM_STATE))
        conv_in = jnp.concatenate([state_conv[l], zx_s[:, :ts, d_ssm:]], axis=1)
        outs["cs"].append(conv_in[:, -(CONV_WIDTH - 1):])

    y_prompt = _rmsnorm(xp, final_norm_g[None, :], tm=256, name="final_norm_p").reshape(bp, tp, d)
    y_sample = _rmsnorm(xs, final_norm_g[None, :], tm=bs * ts, name="final_norm_s").reshape(bs, ts, d)
    st = lambda k: jnp.stack(outs[k])
    return (y_prompt, y_sample, st("kp"), st("vp"), st("hp"), st("cp"), st("ks"), st("vs"), st("hs"), st("cs"))
```

```python
import functools

import jax
import jax.numpy as jnp
from jax import lax
from jax.experimental import pallas as pl
from jax.experimental.pallas import tpu as pltpu

F32 = jnp.float32
BF16 = jnp.bfloat16

SB_HEADS = 16
SB_HEAD_DIM = 128
D_ATTN = SB_HEADS * SB_HEAD_DIM
SSM_HEAD_DIM = 64
HEAD_DIM_SHIFT = 6
SSM_GROUPS = 8
SSM_STATE = 128
CONV_WIDTH = 4
N_MOD = 9
EPS = 1e-6
CHUNK = 128
LANES = 128
SUBLANES = 8
VMEM_LIMIT = 56 * 1024 * 1024

ATTN_BLOCK = 256
ATTN_HEADS_PER_STEP = 4
PAGES_PER_STEP = 4
CONV_ROWS = 512
SSD_GROUPS_PER_STEP = 2


def _params(*sem):
    return pltpu.CompilerParams(dimension_semantics=sem, vmem_limit_bytes=VMEM_LIMIT)


def _wspec(block, imap, layer):
    return pl.BlockSpec((None,) + block, lambda *a: (layer,) + imap(*a))


def _sigmoid(x):
    return 1.0 / (1.0 + jnp.exp(-x))


def _silu(x):
    return x * _sigmoid(x)


def _softplus(x):
    return jnp.maximum(x, 0.0) + jnp.log1p(jnp.exp(-jnp.abs(x)))


def _bdot(a, b):
    return jnp.dot(a.astype(BF16), b.astype(BF16), preferred_element_type=F32)


def _bdot_nt(a, b):
    return lax.dot_general(a.astype(BF16), b.astype(BF16), (((1,), (1,)), ((), ())),
                           preferred_element_type=F32)


def _split3(x):
    hi = x.astype(BF16)
    r1 = x - hi.astype(F32)
    mid = r1.astype(BF16)
    lo = (r1 - mid.astype(F32)).astype(BF16)
    return hi, mid, lo


def _exact_dot(x, m01):
    hi, mid, lo = _split3(x)
    d = functools.partial(jnp.dot, preferred_element_type=F32)
    return d(hi, m01) + d(mid, m01) + d(lo, m01)


def _mm_bias_kernel(x_ref, w_ref, b_ref, *o_refs, pre_silu, out_scale):
    x = x_ref[...]
    if pre_silu:
        x = _silu(x.astype(F32))
    acc = _bdot(x, w_ref[...]) + b_ref[...]
    if out_scale != 1.0:
        acc = acc * out_scale
    for o_ref in o_refs:
        o_ref[...] = acc.astype(o_ref.dtype)


def _mm_bias(x, w, b, layer, *, col0, n, tm, tn, out_dtypes, pre_silu=False, out_scale=1.0, name):
    m, k = x.shape
    c0 = col0 // tn
    outs = pl.pallas_call(
        functools.partial(_mm_bias_kernel, pre_silu=pre_silu, out_scale=out_scale),
        grid=(m // tm, n // tn),
        in_specs=[pl.BlockSpec((tm, k), lambda i, j: (i, 0)),
                  _wspec((k, tn), lambda i, j: (0, c0 + j), layer),
                  pl.BlockSpec((1, tn), lambda i, j: (0, c0 + j))],
        out_specs=[pl.BlockSpec((tm, tn), lambda i, j: (i, j)) for _ in out_dtypes],
        out_shape=[jax.ShapeDtypeStruct((m, n), dt) for dt in out_dtypes],
        compiler_params=_params("parallel", "arbitrary"),
        name=name,
    )(x, w, b)
    return outs[0] if len(outs) == 1 else outs


def _mm_sigmoid_kernel(x_ref, w_ref, b_ref, o_ref):
    acc = _bdot(x_ref[...], w_ref[...])
    o_ref[...] = _sigmoid(acc + b_ref[...]).astype(o_ref.dtype)


def _mm_sigmoid(x, w, b, layer, *, tm, tn, name):
    m, k = x.shape
    n = w.shape[2]
    return pl.pallas_call(
        _mm_sigmoid_kernel,
        grid=(m // tm, n // tn),
        in_specs=[pl.BlockSpec((tm, k), lambda i, j: (i, 0)),
                  _wspec((k, tn), lambda i, j: (0, j), layer),
                  pl.BlockSpec((1, tn), lambda i, j: (0, j))],
        out_specs=pl.BlockSpec((tm, tn), lambda i, j: (i, j)),
        out_shape=jax.ShapeDtypeStruct((m, n), BF16),
        compiler_params=_params("parallel", "arbitrary"),
        name=name,
    )(x, w, b)


def _mm_swiglu_kernel(x_ref, wg_ref, wu_ref, o_ref):
    x = x_ref[...]
    gate = _bdot(x, wg_ref[...])
    up = _bdot(x, wu_ref[...])
    o_ref[...] = (_silu(gate) * up).astype(o_ref.dtype)


def _mm_swiglu(x, w, layer, *, tm, tn, name):
    m, k = x.shape
    f = w.shape[2] // 2
    nb = f // tn
    return pl.pallas_call(
        _mm_swiglu_kernel,
        grid=(m // tm, nb),
        in_specs=[pl.BlockSpec((tm, k), lambda i, j: (i, 0)),
                  _wspec((k, tn), lambda i, j: (0, j), layer),
                  _wspec((k, tn), lambda i, j: (0, nb + j), layer)],
        out_specs=pl.BlockSpec((tm, tn), lambda i, j: (i, j)),
        out_shape=jax.ShapeDtypeStruct((m, f), BF16),
        compiler_params=_params("parallel", "arbitrary"),
        name=name,
    )(x, w, w)


def _mm_resid_kernel(x_ref, w_ref, r_ref, g_ref, o_ref, *, scale):
    acc = _bdot(x_ref[...], w_ref[...])
    o_ref[...] = r_ref[...] + (scale * g_ref[0]) * acc


def _mm_resid(x, w, res, gate, layer, *, scale, tm, tn, rows_per_gate, name, single_buffer_x=False):
    m, k = x.shape
    n = w.shape[2]
    r = gate.shape[1]
    if r == 1:
        assert rows_per_gate % tm == 0
        gmap = lambda i, j: ((i * tm) // rows_per_gate, 0, j)
    else:
        assert r == tm
        gmap = lambda i, j: (i, 0, j)
    x_mode = dict(pipeline_mode=pl.Buffered(1)) if single_buffer_x else {}
    return pl.pallas_call(
        functools.partial(_mm_resid_kernel, scale=scale),
        grid=(m // tm, n // tn),
        in_specs=[pl.BlockSpec((tm, k), lambda i, j: (i, 0), **x_mode),
                  _wspec((k, tn), lambda i, j: (0, j), layer),
                  pl.BlockSpec((tm, tn), lambda i, j: (i, j)),
                  pl.BlockSpec((1, r, tn), gmap)],
        out_specs=pl.BlockSpec((tm, tn), lambda i, j: (i, j)),
        out_shape=jax.ShapeDtypeStruct((m, n), F32),
        compiler_params=_params("parallel", "arbitrary"),
        name=name,
    )(x, w, res, gate)


def _mm_merge_kernel(oa_ref, wa_ref, os_ref, ws_ref, ga_ref, gs_ref, o_ref):
    a = _bdot(oa_ref[...], wa_ref[...])
    s = _bdot(os_ref[...], ws_ref[...])
    o_ref[...] = (ga_ref[...].astype(F32) * a + gs_ref[...].astype(F32) * s).astype(o_ref.dtype)


def _mm_merge(oa, wa, os_, ws, gates, layer, *, tm, tn, name):
    m = oa.shape[0]
    n = wa.shape[2]
    nb = n // tn
    return pl.pallas_call(
        _mm_merge_kernel,
        grid=(m // tm, nb),
        in_specs=[pl.BlockSpec((tm, oa.shape[1]), lambda i, j: (i, 0)),
                  _wspec((wa.shape[1], tn), lambda i, j: (0, j), layer),
                  pl.BlockSpec((tm, os_.shape[1]), lambda i, j: (i, 0)),
                  _wspec((ws.shape[1], tn), lambda i, j: (0, j), layer),
                  pl.BlockSpec((tm, tn), lambda i, j: (i, j)),
                  pl.BlockSpec((tm, tn), lambda i, j: (i, nb + j))],
        out_specs=pl.BlockSpec((tm, tn), lambda i, j: (i, j)),
        out_shape=jax.ShapeDtypeStruct((m, n), BF16),
        compiler_params=_params("parallel", "arbitrary"),
        name=name,
    )(oa, wa, os_, ws, gates, gates)


def _modnorm_kernel(x_ref, g_ref, sh_ref, sc_ref, o_ref):
    x = x_ref[...]
    y = x * lax.rsqrt(jnp.mean(x * x, axis=-1, keepdims=True) + EPS)
    y = y * g_ref[...]
    o_ref[...] = (y * (1.0 + sc_ref[0]) + sh_ref[0]).astype(o_ref.dtype)


def _modnorm(x, g, shift, scale, *, rows_per_seq, tm, name):
    m, d = x.shape
    r = shift.shape[1]
    if r == 1:
        assert rows_per_seq % tm == 0
        smap = lambda i: ((i * tm) // rows_per_seq, 0, 0)
    else:
        assert r == tm
        smap = lambda i: (i, 0, 0)
    return pl.pallas_call(
        _modnorm_kernel,
        grid=(m // tm,),
        in_specs=[pl.BlockSpec((tm, d), lambda i: (i, 0)),
                  pl.BlockSpec((1, d), lambda i: (0, 0)),
                  pl.BlockSpec((1, r, d), smap),
                  pl.BlockSpec((1, r, d), smap)],
        out_specs=pl.BlockSpec((tm, d), lambda i: (i, 0)),
        out_shape=jax.ShapeDtypeStruct((m, d), BF16),
        compiler_params=_params("parallel"),
        name=name,
    )(x, g, shift, scale)


def _rmsnorm_kernel(x_ref, g_ref, o_ref):
    x = x_ref[...]
    y = x * lax.rsqrt(jnp.mean(x * x, axis=-1, keepdims=True) + EPS)
    o_ref[...] = y * g_ref[...]


def _rmsnorm(x, g, *, tm, name):
    m, d = x.shape
    return pl.pallas_call(
        _rmsnorm_kernel,
        grid=(m // tm,),
        in_specs=[pl.BlockSpec((tm, d), lambda i: (i, 0)),
                  pl.BlockSpec((1, d), lambda i: (0, 0))],
        out_specs=pl.BlockSpec((tm, d), lambda i: (i, 0)),
        out_shape=jax.ShapeDtypeStruct((m, d), F32),
        compiler_params=_params("parallel"),
        name=name,
    )(x, g)


def _cumsum_rhs():
    j = lax.broadcasted_iota(jnp.int32, (2 * LANES, 2 * LANES), 0) & (LANES - 1)
    s = lax.broadcasted_iota(jnp.int32, (2 * LANES, 2 * LANES), 1)
    return jnp.where((j > s) | (s >= LANES), 1.0, 0.0).astype(BF16)


def _sb_weights(z_tiles, vis_tiles, carry, rhs):
    ls_t, part_t = [], []
    for z, vis in zip(z_tiles, vis_tiles):
        t = jnp.log(1.0 + jnp.exp(-jnp.abs(z)))
        ls_t.append(jnp.minimum(z, 0.0) - t)
        lk = -jnp.maximum(z, 0.0) - t
        if vis is not None:
            lk = jnp.where(vis, lk, 0.0)
        hi = lk.astype(BF16)
        lo = (lk - hi.astype(F32)).astype(BF16)
        part_t.append(jnp.dot(jnp.concatenate([hi, lo], axis=1), rhs, preferred_element_type=F32))
    w_t = [None] * len(z_tiles)
    for i in range(len(z_tiles) - 1, -1, -1):
        later = carry + part_t[i][:, :LANES]
        w = jnp.exp(ls_t[i] + later)
        if vis_tiles[i] is not None:
            w = jnp.where(vis_tiles[i], w, 0.0)
        w_t[i] = w
        carry = carry + part_t[i][:, LANES:]
    return w_t, carry


def _attn_prompt_kernel(bias_ref, q_ref, k_ref, v_ref, o_ref, acc_ref, carry_ref):
    hg = pl.program_id(1)
    qi = pl.program_id(2)
    nh = ATTN_HEADS_PER_STEP
    nt = ATTN_BLOCK // LANES
    rhs = _cumsum_rhs()
    row = lax.broadcasted_iota(jnp.int32, (ATTN_BLOCK, LANES), 0)
    col = lax.broadcasted_iota(jnp.int32, (ATTN_BLOCK, LANES), 1)
    diag_vis = [col + t * LANES < row for t in range(nt)]

    def hcols(h):
        return slice(h * SB_HEAD_DIM, (h + 1) * SB_HEAD_DIM)

    def block(kb, vis_tiles, first):
        start = pl.multiple_of(kb * ATTN_BLOCK, ATTN_BLOCK)
        for h in range(nh):
            bias = bias_ref[hg * nh + h]
            q = q_ref[0, :, hcols(h)]
            kblk = k_ref[0, pl.ds(start, ATTN_BLOCK), hcols(h)]
            vblk = v_ref[0, pl.ds(start, ATTN_BLOCK), hcols(h)]
            z = _bdot_nt(q, kblk) + bias
            carry = jnp.zeros((ATTN_BLOCK, LANES), F32) if first else carry_ref[h]
            w_t, carry = _sb_weights([z[:, t * LANES:(t + 1) * LANES] for t in range(nt)], vis_tiles, carry, rhs)
            carry_ref[h] = carry
            pv = jnp.dot(jnp.concatenate([w.astype(BF16) for w in w_t], axis=1), vblk, preferred_element_type=F32)
            acc_ref[h] = pv if first else acc_ref[h] + pv

    block(qi, diag_vis, True)

    def body(it, _):
        block(qi - 1 - it, [None] * nt, False)
        return 0

    lax.fori_loop(0, qi, body, 0)
    for h in range(nh):
        o_ref[0, :, hcols(h)] = acc_ref[h].astype(o_ref.dtype)


def _attn_prompt(q, k, v, sb_bias):
    b, t, _ = q.shape
    nq = t // ATTN_BLOCK
    nh = ATTN_HEADS_PER_STEP
    w = nh * SB_HEAD_DIM
    return pl.pallas_call(
        _attn_prompt_kernel,
        grid=(b, SB_HEADS // nh, nq),
        in_specs=[pl.BlockSpec(memory_space=pltpu.SMEM),
                  pl.BlockSpec((1, ATTN_BLOCK, w), lambda bi, h, qi: (bi, qi, h)),
                  pl.BlockSpec((1, t, w), lambda bi, h, qi: (bi, 0, h)),
                  pl.BlockSpec((1, t, w), lambda bi, h, qi: (bi, 0, h))],
        out_specs=pl.BlockSpec((1, ATTN_BLOCK, w), lambda bi, h, qi: (bi, qi, h)),
        out_shape=jax.ShapeDtypeStruct((b, t, D_ATTN), BF16),
        scratch_shapes=[pltpu.VMEM((nh, ATTN_BLOCK, SB_HEAD_DIM), F32),
                        pltpu.VMEM((nh, ATTN_BLOCK, LANES), F32)],
        compiler_params=_params("parallel", "parallel", "arbitrary"),
        name="attn_prompt",
    )(sb_bias, q, k, v)


def _attn_sample_kernel(pt_ref, bias_ref, q_ref, kn_ref, vn_ref, *refs, tq):
    npg = PAGES_PER_STEP
    kc_refs, vc_refs = refs[:npg], refs[npg:2 * npg]
    o_ref, acc_ref, carry_ref = refs[2 * npg:]
    s = pl.program_id(1)
    rows = SB_HEADS * tq
    rhs = _cumsum_rhs()

    def hcols(h):
        return slice(h * SB_HEAD_DIM, (h + 1) * SB_HEAD_DIM)

    def process(k_tiles, v_tiles, vis):
        z_t = [jnp.concatenate([_bdot_nt(q_ref[:, hcols(h)], k_of(h)) + bias_ref[h] for h in range(SB_HEADS)], axis=0)
               for k_of in k_tiles]
        w_t, carry = _sb_weights(z_t, [vis] * len(z_t), carry_ref[...], rhs)
        carry_ref[...] = carry
        for h in range(SB_HEADS):
            w_h = jnp.concatenate([w[h * tq:(h + 1) * tq, :].astype(BF16) for w in w_t], axis=1)
            v_h = jnp.concatenate([v_of(h).astype(BF16) for v_of in v_tiles], axis=0)
            acc_ref[h * tq:(h + 1) * tq, :] += jnp.dot(w_h, v_h, preferred_element_type=F32)

    @pl.when(s == 0)
    def _():
        carry_ref[...] = jnp.zeros_like(carry_ref)
        acc_ref[...] = jnp.zeros_like(acc_ref)
        pad = jnp.zeros((CHUNK - tq, SB_HEAD_DIM), F32)
        r_idx = lax.broadcasted_iota(jnp.int32, (rows, CHUNK), 0) & (tq - 1)
        s_idx = lax.broadcasted_iota(jnp.int32, (rows, CHUNK), 1)
        process([lambda h: jnp.concatenate([kn_ref[:, hcols(h)], pad], axis=0)],
                [lambda h: jnp.concatenate([vn_ref[:, hcols(h)], pad], axis=0)],
                s_idx < r_idx)

    def head_rows(r, h):
        return r[pl.ds(h, CHUNK, stride=SB_HEADS), :]

    process([functools.partial(head_rows, r) for r in reversed(kc_refs)],
            [functools.partial(head_rows, r) for r in reversed(vc_refs)], None)

    @pl.when(s == pl.num_programs(1) - 1)
    def _():
        for h in range(SB_HEADS):
            o_ref[:, hcols(h)] = acc_ref[h * tq:(h + 1) * tq, :].astype(o_ref.dtype)


def _attn_sample(q, kn, vn, cache_k, cache_v, page_table, sb_bias, layer, *, tq):
    bs, n_pages = page_table.shape
    page_rows = cache_k.shape[2]
    npg = PAGES_PER_STEP
    assert page_rows == CHUNK * SB_HEADS and tq == SUBLANES and n_pages % npg == 0
    new_spec = pl.BlockSpec((tq, D_ATTN), lambda b, s, pt: (b, 0))

    def page_spec(i):
        return pl.BlockSpec((None, None, page_rows, SB_HEAD_DIM),
                            lambda b, s, pt: (layer, pt[b, n_pages - 1 - (s * npg + i)], 0, 0))

    return pl.pallas_call(
        functools.partial(_attn_sample_kernel, tq=tq),
        grid_spec=pltpu.PrefetchScalarGridSpec(
            num_scalar_prefetch=1,
            grid=(bs, n_pages // npg),
            in_specs=[pl.BlockSpec(memory_space=pltpu.SMEM), new_spec, new_spec, new_spec]
                     + [page_spec(i) for i in range(npg)] * 2,
            out_specs=pl.BlockSpec((tq, D_ATTN), lambda b, s, pt: (b, 0)),
            scratch_shapes=[pltpu.VMEM((SB_HEADS * tq, SB_HEAD_DIM), F32),
                            pltpu.VMEM((SB_HEADS * tq, LANES), F32)],
        ),
        out_shape=jax.ShapeDtypeStruct((bs * tq, D_ATTN), F32),
        compiler_params=_params("parallel", "arbitrary"),
        name="attn_sample",
    )(page_table, sb_bias, q, kn, vn, *([cache_k] * npg), *([cache_v] * npg))


def _conv_kernel(cur_ref, prev_ref, buf_ref, w_ref, b_ref, o_ref, xp_ref):
    c = pl.program_id(1)
    rows = cur_ref.shape[1]
    prev = jnp.where(c == 0, buf_ref[0], prev_ref[0])
    xp_ref[0:SUBLANES, :] = prev
    xp_ref[SUBLANES:, :] = cur_ref[0]
    out = b_ref[...] + xp_ref[pl.ds(SUBLANES - CONV_WIDTH + 1, rows), :] * w_ref[0:1, :]
    for i in range(1, CONV_WIDTH):
        out = out + xp_ref[pl.ds(SUBLANES - CONV_WIDTH + 1 + i, rows), :] * w_ref[i:i + 1, :]
    o_ref[0] = _silu(out)


def _conv(zx, buf8, conv_w, conv_b, *, col0, tc, name):
    b, t, _ = zx.shape
    ch = conv_w.shape[1]
    c0 = col0 // tc
    rows = min(t, CONV_ROWS)
    per = rows // SUBLANES
    return pl.pallas_call(
        _conv_kernel,
        grid=(b, t // rows, ch // tc),
        in_specs=[pl.BlockSpec((1, rows, tc), lambda bi, c, j: (bi, c, c0 + j)),
                  pl.BlockSpec((1, SUBLANES, tc), lambda bi, c, j: (bi, jnp.maximum(c * per - 1, 0), c0 + j)),
                  pl.BlockSpec((1, SUBLANES, tc), lambda bi, c, j: (bi, 0, j)),
                  pl.BlockSpec((CONV_WIDTH, tc), lambda bi, c, j: (0, j)),
                  pl.BlockSpec((1, tc), lambda bi, c, j: (0, j))],
        out_specs=pl.BlockSpec((1, rows, tc), lambda bi, c, j: (bi, c, j)),
        out_shape=jax.ShapeDtypeStruct((b, t, ch), F32),
        scratch_shapes=[pltpu.VMEM((rows + SUBLANES, tc), F32)],
        compiler_params=_params("parallel", "parallel", "parallel"),
        name=name,
    )(zx, zx, buf8, conv_w, conv_b)


def _ssd_kernel(xs_ref, bm_ref, cm_ref, z_ref, dt_ref, dtt_ref, h0_ref, dtb_row_ref, dtb_col_ref, alog_row_ref,
                alog_col_ref, dskip_ref, ng_ref, y_ref, hout_ref, ht_ref, *, t_valid):
    c = pl.program_id(2)
    gps = bm_ref.shape[2] // SSM_STATE
    gw = xs_ref.shape[2] // gps
    hpg = gw // SSM_HEAD_DIM
    f32dot = functools.partial(jnp.dot, preferred_element_type=F32)

    @pl.when(c == 0)
    def _():
        ht_ref[...] = h0_ref[0].T

    row = lax.broadcasted_iota(jnp.int32, (CHUNK, CHUNK), 0)
    col = lax.broadcasted_iota(jnp.int32, (CHUNK, CHUNK), 1)
    causal = row >= col
    tri_incl = jnp.where(causal, 1.0, 0.0).astype(BF16)
    tri_incl_t = jnp.where(col >= row, 1.0, 0.0).astype(BF16)
    e_row = lax.broadcasted_iota(jnp.int32, (LANES, gw), 0)
    e_col = lax.broadcasted_iota(jnp.int32, (LANES, gw), 1)
    t_col = c * CHUNK + lax.broadcasted_iota(jnp.int32, (CHUNK, gw), 0)
    t_row = c * CHUNK + lax.broadcasted_iota(jnp.int32, (hpg, CHUNK), 1)
    lane = lax.broadcasted_iota(jnp.int32, (CHUNK, LANES), 1)
    first_head = lane < SSM_HEAD_DIM

    for u in range(gps):
        g = pl.program_id(1) * gps + u
        ch = slice(u * gw, (u + 1) * gw)
        st = slice(u * SSM_STATE, (u + 1) * SSM_STATE)

        expand = jnp.where(e_row == g * hpg + lax.shift_right_logical(e_col, HEAD_DIM_SHIFT), 1.0, 0.0).astype(BF16)

        dt_raw = _exact_dot(dt_ref[0], expand)
        dt = jnp.where(t_col < t_valid, _softplus(dt_raw + dtb_row_ref[:, ch]), 0.0)
        a_row = -jnp.exp(alog_row_ref[:, ch])
        hi, mid, lo = _split3(dt * a_row)
        a_cum = f32dot(tri_incl, hi) + f32dot(tri_incl, mid) + f32dot(tri_incl, lo)
        a_end = a_cum[CHUNK - 1:CHUNK, :]

        h_lo = pl.multiple_of(g * hpg, hpg)
        dt_r = jnp.where(t_row < t_valid,
                         _softplus(dtt_ref[0, pl.ds(h_lo, hpg), :] + dtb_col_ref[pl.ds(h_lo, hpg), :]), 0.0)
        hi, mid, lo = _split3(dt_r * (-jnp.exp(alog_col_ref[pl.ds(h_lo, hpg), :])))
        a_cum_r = f32dot(hi, tri_incl_t) + f32dot(mid, tri_incl_t) + f32dot(lo, tri_incl_t)

        xs = xs_ref[0, :, ch]
        bm = bm_ref[0, :, st]
        cm = cm_ref[0, :, st].astype(BF16)
        xdt = xs * dt
        xdt_b = xdt.astype(BF16)
        cb = _bdot_nt(cm, bm)

        ht = ht_ref[:, ch]
        y = f32dot(cm, ht.astype(BF16)) * jnp.exp(a_cum)
        y_diag = []
        for p in range(hpg // 2):
            cols = slice(p * LANES, (p + 1) * LANES)
            scores = []
            for j in (2 * p, 2 * p + 1):
                a_l = jnp.broadcast_to(a_cum[:, j * SSM_HEAD_DIM:j * SSM_HEAD_DIM + 1], (CHUNK, CHUNK))
                seg = a_l - a_cum_r[j:j + 1, :]
                scores.append((cb * jnp.exp(jnp.where(causal, seg, -jnp.inf))).astype(BF16))
            xp = xdt_b[:, cols]
            rhs = jnp.concatenate([jnp.where(first_head, xp, jnp.zeros_like(xp)),
                                   jnp.where(first_head, jnp.zeros_like(xp), xp)], axis=0)
            y_diag.append(f32dot(jnp.concatenate(scores, axis=1), rhs))
        y = y + jnp.concatenate(y_diag, axis=1)

        xd = (xdt * jnp.exp(a_end - a_cum)).astype(BF16)
        ht_ref[:, ch] = ht * jnp.exp(a_end) + f32dot(bm.T.astype(BF16), xd)

        y = y + xs * dskip_ref[:, ch]
        y = y * _silu(z_ref[0, :, ch])
        y = y * lax.rsqrt(jnp.mean(y * y, axis=-1, keepdims=True) + EPS)
        y_ref[0, :, ch] = (y * ng_ref[:, ch]).astype(y_ref.dtype)

    @pl.when(c == pl.num_programs(2) - 1)
    def _():
        hout_ref[0] = ht_ref[...].T


def _ssd(xc, zx, dt, dtt, h0, dtb_row, dtb_col, alog_row, alog_col, dskip, ng, *, t_valid, name):
    b, t, cc = xc.shape
    d_ssm = cc - 2 * SSM_GROUPS * SSM_STATE
    gps = SSD_GROUPS_PER_STEP
    gw = gps * d_ssm // SSM_GROUPS
    sw = gps * SSM_STATE
    nb0 = d_ssm // sw
    assert d_ssm % sw == 0 and SSM_GROUPS % gps == 0
    gmap = lambda bi, g, c: (0, g)
    return pl.pallas_call(
        functools.partial(_ssd_kernel, t_valid=t_valid),
        grid=(b, SSM_GROUPS // gps, t // CHUNK),
        in_specs=[pl.BlockSpec((1, CHUNK, gw), lambda bi, g, c: (bi, c, g)),
                  pl.BlockSpec((1, CHUNK, sw), lambda bi, g, c: (bi, c, nb0 + g)),
                  pl.BlockSpec((1, CHUNK, sw), lambda bi, g, c: (bi, c, nb0 + SSM_GROUPS // gps + g)),
                  pl.BlockSpec((1, CHUNK, gw), lambda bi, g, c: (bi, c, g)),
                  pl.BlockSpec((1, CHUNK, LANES), lambda bi, g, c: (bi, c, 0)),
                  pl.BlockSpec((1, LANES, CHUNK), lambda bi, g, c: (bi, 0, c)),
                  pl.BlockSpec((1, gw, SSM_STATE), lambda bi, g, c: (bi, g, 0)),
                  pl.BlockSpec((1, gw), gmap),
                  pl.BlockSpec((LANES, 1), lambda bi, g, c: (0, 0)),
                  pl.BlockSpec((1, gw), gmap),
                  pl.BlockSpec((LANES, 1), lambda bi, g, c: (0, 0)),
                  pl.BlockSpec((1, gw), gmap),
                  pl.BlockSpec((1, gw), gmap)],
        out_specs=[pl.BlockSpec((1, CHUNK, gw), lambda bi, g, c: (bi, c, g)),
                   pl.BlockSpec((1, gw, SSM_STATE), lambda bi, g, c: (bi, g, 0))],
        out_shape=[jax.ShapeDtypeStruct((b, t, d_ssm), BF16),
                   jax.ShapeDtypeStruct((b, d_ssm, SSM_STATE), F32)],
        scratch_shapes=[pltpu.VMEM((SSM_STATE, gw), F32)],
        compiler_params=_params("parallel", "parallel", "arbitrary"),
        name=name,
    )(xc, xc, xc, zx, dt, dtt, h0, dtb_row, dtb_col, alog_row, alog_col, dskip, ng)


def _pad_heads(v):
    return jnp.pad(v, (0, LANES - v.shape[0]))


def _layer(x, mod, attn_fn, h0, conv_buf, wts, small, layer, *, rows_per_seq, tm, tn_ffn, q_dtype, kv_dtypes, tag):
    m, d = x.shape
    nseq = m // rows_per_seq
    (w_ffn1_in, w_ffn1_out, w_in, w_gate, w_branch_attn, w_branch_ssm, w_out, w_ffn2_in, w_ffn2_out) = wts
    (norm_g, w_dt, conv_w, conv_b, dt_bias, a_log, d_skip, ssm_norm_g, b_gate) = small
    per_seq = rows_per_seq % tm == 0
    tnorm = 256 if per_seq else m
    big_ffn_out = tm * w_ffn1_out.shape[1] * 2 > 16 * 1024 * 1024
    d_ssm = w_branch_ssm.shape[1]
    conv_ch = conv_w.shape[1]
    rep = lambda v: jnp.repeat(v, SSM_HEAD_DIM)[None, :]
    nm = lambda s: f"{s}_{tag}"

    def mvec(idx):
        v = mod[:, idx:idx + 1, :]
        return v if per_seq else jnp.repeat(v[:, 0, :], rows_per_seq, axis=0)[None]

    def modnorm(x, k):
        return _modnorm(x, norm_g[k:k + 1], mvec(3 * k), mvec(3 * k + 1), rows_per_seq=rows_per_seq, tm=tnorm,
                        name=nm(f"modnorm{k}"))

    def ffn(x, k, w_a, w_b):
        hdn = _mm_swiglu(modnorm(x, k), w_a, layer, tm=tm, tn=tn_ffn, name=nm(f"ffn{k}_in"))
        return _mm_resid(hdn, w_b, x, mvec(3 * k + 2), layer, scale=0.5, tm=tm, tn=tn_ffn,
                         rows_per_gate=rows_per_seq, name=nm(f"ffn{k}_out"), single_buffer_x=big_ffn_out)

    x = ffn(x, 0, w_ffn1_in, w_ffn1_out)

    n = modnorm(x, 1)
    zero_bias = jnp.zeros((1, w_in.shape[2]), F32)
    proj = functools.partial(_mm_bias, n, w_in, zero_bias, layer, tm=tm, tn=512)
    q = proj(col0=0, n=D_ATTN, out_dtypes=(q_dtype,), out_scale=SB_HEAD_DIM ** -0.5, name=nm("proj_q"))
    k = proj(col0=D_ATTN, n=D_ATTN, out_dtypes=kv_dtypes, name=nm("proj_k"))
    v = proj(col0=2 * D_ATTN, n=D_ATTN, out_dtypes=kv_dtypes, name=nm("proj_v"))
    zx = proj(col0=3 * D_ATTN, n=d_ssm + conv_ch, out_dtypes=(F32,), name=nm("proj_zx"))
    dt_raw = _mm_bias(n, w_dt, zero_bias[:, :LANES], 0, col0=0, n=LANES, tm=tm, tn=LANES, out_dtypes=(F32,),
                      name=nm("proj_dt"))
    gates = _mm_sigmoid(n, w_gate, b_gate[None, :], layer, tm=tm, tn=512, name=nm("gates"))

    o_attn = attn_fn(q, k, v)

    t_pad = -(-rows_per_seq // CHUNK) * CHUNK
    zx3 = zx.reshape(nseq, rows_per_seq, d_ssm + conv_ch)
    dt3 = dt_raw.reshape(nseq, rows_per_seq, LANES)
    if t_pad != rows_per_seq:
        padt = ((0, 0), (0, t_pad - rows_per_seq), (0, 0))
        zx3, dt3 = jnp.pad(zx3, padt), jnp.pad(dt3, padt)
    buf8 = jnp.pad(conv_buf, ((0, 0), (SUBLANES - conv_buf.shape[1], 0), (0, 0)))
    xc = _conv(zx3, buf8, conv_w, conv_b[None, :], col0=d_ssm, tc=1024, name=nm("conv"))
    y_ssm, h_new = _ssd(xc, zx3, dt3, jnp.swapaxes(dt3, 1, 2), h0,
                        rep(dt_bias), _pad_heads(dt_bias)[:, None], rep(a_log), _pad_heads(a_log)[:, None],
                        rep(d_skip), ssm_norm_g[None, :], t_valid=rows_per_seq, name=nm("ssd"))
    o_ssm = y_ssm[:, :rows_per_seq].reshape(m, d_ssm)

    merged = _mm_merge(o_attn, w_branch_attn, o_ssm, w_branch_ssm, gates, layer, tm=tm, tn=256, name=nm("merge"))
    x = _mm_resid(merged, w_out, x, mvec(5), layer, scale=1.0, tm=tm, tn=512, rows_per_gate=rows_per_seq,
                  name=nm("mix_out"))

    x = ffn(x, 2, w_ffn2_in, w_ffn2_out)
    return x, k, v, zx3, h_new


def kernel(x_prompt, x_sample, c_prompt, c_sample, cache_k, cache_v, page_table, state_ssm, state_conv, norm_g,
           w_mod, b_mod, w_ffn1_in, w_ffn1_out, w_in, sb_bias, conv_w, conv_b, dt_bias, a_log, d_skip, ssm_norm_g,
           w_gate, b_gate, w_branch_attn, w_branch_ssm, w_out, w_ffn2_in, w_ffn2_out, final_norm_g):
    bp, tp, d = x_prompt.shape
    bs, ts, _ = x_sample.shape
    depth = norm_g.shape[0]
    n_pool, page = cache_k.shape[1], cache_k.shape[2]
    d_ssm = w_branch_ssm.shape[1]
    n_heads = dt_bias.shape[1]
    conv_ch = conv_w.shape[2]
    n_main = 3 * D_ATTN + d_ssm + conv_ch

    xp = x_prompt.reshape(bp * tp, d)
    xs = x_sample.reshape(bs * ts, d)
    c_all = jnp.concatenate([c_prompt, c_sample], axis=0)
    c_rows = -(-c_all.shape[0] // SUBLANES) * SUBLANES
    c_all = jnp.pad(c_all, ((0, c_rows - c_all.shape[0]), (0, 0)))
    h0_prompt = jnp.zeros((bp, d_ssm, SSM_STATE), F32)
    conv0_prompt = jnp.zeros((bp, CONV_WIDTH - 1, conv_ch), F32)
    ck = cache_k.reshape(depth, n_pool, page * SB_HEADS, SB_HEAD_DIM)
    cv = cache_v.reshape(depth, n_pool, page * SB_HEADS, SB_HEAD_DIM)
    wts = (w_ffn1_in, w_ffn1_out, w_in, w_gate, w_branch_attn, w_branch_ssm, w_out, w_ffn2_in, w_ffn2_out)

    outs = {k: [] for k in ("kp", "vp", "hp", "cp", "ks", "vs", "hs", "cs")}
    for l in range(depth):
        mod = _mm_bias(c_all, w_mod, b_mod[l][None, :], l, col0=0, n=N_MOD * d, tm=c_rows, tn=512,
                       out_dtypes=(F32,), pre_silu=True, name=f"mod_l{l}").reshape(c_rows, N_MOD, d)
        w_dt = jnp.pad(w_in[l, :, n_main:], ((0, 0), (0, LANES - n_heads)))[None]
        small = (norm_g[l], w_dt, conv_w[l], conv_b[l], dt_bias[l], a_log[l], d_skip[l], ssm_norm_g[l], b_gate[l])

        def attn_p(q, k, v):
            r3 = lambda a: a.reshape(bp, tp, D_ATTN)
            return _attn_prompt(r3(q), r3(k[1]), r3(v[1]), sb_bias[l]).reshape(bp * tp, D_ATTN)

        xp, k_p, v_p, zx_p, h_p = _layer(xp, mod[:bp], attn_p, h0_prompt, conv0_prompt, wts, small, l,
                                         rows_per_seq=tp, tm=min(tp, 1024), tn_ffn=256, q_dtype=BF16,
                                         kv_dtypes=(F32, BF16), tag=f"p{l}")
        outs["kp"].append(k_p[0].reshape(bp, tp, SB_HEADS, SB_HEAD_DIM))
        outs["vp"].append(v_p[0].reshape(bp, tp, SB_HEADS, SB_HEAD_DIM))
        outs["hp"].append(h_p.reshape(bp, n_heads, SSM_HEAD_DIM, SSM_STATE))
        outs["cp"].append(zx_p[:, tp - (CONV_WIDTH - 1):, d_ssm:])

        def attn_s(q, k, v):
            return _attn_sample(q, k, v, ck, cv, page_table, sb_bias[l], l, tq=ts)

        xs, k_s, v_s, zx_s, h_s = _layer(xs, mod[bp:bp + bs], attn_s, state_ssm[l].reshape(bs, d_ssm, SSM_STATE),
                                         state_conv[l], wts, small, l, rows_per_seq=ts, tm=bs * ts, tn_ffn=256,
                                         q_dtype=F32, kv_dtypes=(F32,), tag=f"s{l}")
        outs["ks"].append(k_s.reshape(bs, ts, SB_HEADS, SB_HEAD_DIM))
        outs["vs"].append(v_s.reshape(bs, ts, SB_HEADS, SB_HEAD_DIM))
        outs["hs"].append(h_s.reshape(bs, n_heads, SSM_HEAD_DIM, SSM_STATE))
        conv_in = jnp.concatenate([state_conv[l], zx_s[:, :ts, d_ssm:]], axis=1)
        outs["cs"].append(conv_in[:, -(CONV_WIDTH - 1):])

    y_prompt = _rmsnorm(xp, final_norm_g[None, :], tm=256, name="final_norm_p").reshape(bp, tp, d)
    y_sample = _rmsnorm(xs, final_norm_g[None, :], tm=bs * ts, name="final_norm_s").reshape(bs, ts, d)
    st = lambda k: jnp.stack(outs[k])
    return (y_prompt, y_sample, st("kp"), st("vp"), st("hp"), st("cp"), st("ks"), st("vs"), st("hs"), st("cs"))
```

```python
import functools

import jax
import jax.numpy as jnp
from jax import lax
from jax.experimental import pallas as pl
from jax.experimental.pallas import tpu as pltpu

F32 = jnp.float32
BF16 = jnp.bfloat16

SB_HEADS = 16
HEADS_SHIFT = 4
TQ_SHIFT = 3
SB_HEAD_DIM = 128
D_ATTN = SB_HEADS * SB_HEAD_DIM
SSM_HEAD_DIM = 64
HEAD_DIM_SHIFT = 6
SSM_GROUPS = 8
SSM_STATE = 128
CONV_WIDTH = 4
N_MOD = 9
EPS = 1e-6
CHUNK = 128
LANES = 128
SUBLANES = 8
VMEM_LIMIT = 56 * 1024 * 1024

ATTN_BLOCK = 256
ATTN_HEADS_PER_STEP = 4
PAGES_PER_STEP = 4
CONV_ROWS = 512
SSD_GROUPS_PER_STEP = 4


def _params(*sem):
    return pltpu.CompilerParams(dimension_semantics=sem, vmem_limit_bytes=VMEM_LIMIT)


def _wspec(block, imap, layer):
    return pl.BlockSpec((None,) + block, lambda *a: (layer,) + imap(*a))


def _sigmoid(x):
    return 1.0 / (1.0 + jnp.exp(-x))


def _silu(x):
    return x * _sigmoid(x)


def _softplus(x):
    return jnp.maximum(x, 0.0) + jnp.log1p(jnp.exp(-jnp.abs(x)))


def _bdot(a, b):
    return jnp.dot(a.astype(BF16), b.astype(BF16), preferred_element_type=F32)


def _bdot_nt(a, b):
    return lax.dot_general(a.astype(BF16), b.astype(BF16), (((1,), (1,)), ((), ())),
                           preferred_element_type=F32)


def _split3(x):
    hi = x.astype(BF16)
    r1 = x - hi.astype(F32)
    mid = r1.astype(BF16)
    lo = (r1 - mid.astype(F32)).astype(BF16)
    return hi, mid, lo


def _exact_dot(x, m01):
    hi, mid, lo = _split3(x)
    d = functools.partial(jnp.dot, preferred_element_type=F32)
    return d(hi, m01) + d(mid, m01) + d(lo, m01)


def _mm_bias_kernel(x_ref, w_ref, b_ref, o_ref, *, pre_silu):
    x = x_ref[...]
    if pre_silu:
        x = _silu(x.astype(F32))
    o_ref[...] = _bdot(x, w_ref[...]) + b_ref[...]


def _mm_bias(x, w, b, layer, *, col0, n, tm, tn, pre_silu=False, name):
    m, k = x.shape
    c0 = col0 // tn
    return pl.pallas_call(
        functools.partial(_mm_bias_kernel, pre_silu=pre_silu),
        grid=(m // tm, n // tn),
        in_specs=[pl.BlockSpec((tm, k), lambda i, j: (i, 0)),
                  _wspec((k, tn), lambda i, j: (0, c0 + j), layer),
                  pl.BlockSpec((1, tn), lambda i, j: (0, c0 + j))],
        out_specs=pl.BlockSpec((tm, tn), lambda i, j: (i, j)),
        out_shape=jax.ShapeDtypeStruct((m, n), F32),
        compiler_params=_params("parallel", "arbitrary"),
        name=name,
    )(x, w, b)


def _mm_qkv_kernel(x_ref, w_ref, *o_refs, nq, scale):
    j = pl.program_id(1)
    acc = _bdot(x_ref[...], w_ref[...])
    o_refs[0][...] = (acc * jnp.where(j < nq, scale, 1.0)).astype(o_refs[0].dtype)
    if len(o_refs) > 1:
        @pl.when(j >= nq)
        def _():
            o_refs[1][...] = acc


def _mm_qkv(x, w, layer, *, tm, tn, scale, split, name):
    m, k = x.shape
    nq = D_ATTN // tn
    out_specs = [pl.BlockSpec((tm, tn), lambda i, j: (i, j))]
    out_shape = [jax.ShapeDtypeStruct((m, 3 * D_ATTN), BF16 if split else F32)]
    if split:
        out_specs.append(pl.BlockSpec((tm, tn), lambda i, j: (i, jnp.maximum(j - nq, 0))))
        out_shape.append(jax.ShapeDtypeStruct((m, 2 * D_ATTN), F32))
    outs = pl.pallas_call(
        functools.partial(_mm_qkv_kernel, nq=nq, scale=scale),
        grid=(m // tm, 3 * nq),
        in_specs=[pl.BlockSpec((tm, k), lambda i, j: (i, 0)),
                  _wspec((k, tn), lambda i, j: (0, j), layer)],
        out_specs=out_specs,
        out_shape=out_shape,
        compiler_params=_params("parallel", "arbitrary"),
        name=name,
    )(x, w)
    return outs if split else outs[0]


def _mm_sigmoid_kernel(x_ref, w_ref, b_ref, o_ref):
    acc = _bdot(x_ref[...], w_ref[...])
    o_ref[...] = _sigmoid(acc + b_ref[...]).astype(o_ref.dtype)


def _mm_sigmoid(x, w, b, layer, *, tm, tn, name):
    m, k = x.shape
    n = w.shape[2]
    return pl.pallas_call(
        _mm_sigmoid_kernel,
        grid=(m // tm, n // tn),
        in_specs=[pl.BlockSpec((tm, k), lambda i, j: (i, 0)),
                  _wspec((k, tn), lambda i, j: (0, j), layer),
                  pl.BlockSpec((1, tn), lambda i, j: (0, j))],
        out_specs=pl.BlockSpec((tm, tn), lambda i, j: (i, j)),
        out_shape=jax.ShapeDtypeStruct((m, n), BF16),
        compiler_params=_params("parallel", "arbitrary"),
        name=name,
    )(x, w, b)


def _mm_swiglu_kernel(x_ref, wg_ref, wu_ref, o_ref):
    x = x_ref[...]
    gate = _bdot(x, wg_ref[...])
    up = _bdot(x, wu_ref[...])
    o_ref[...] = (_silu(gate) * up).astype(o_ref.dtype)


def _mm_swiglu(x, w, layer, *, tm, tn, name):
    m, k = x.shape
    f = w.shape[2] // 2
    nb = f // tn
    return pl.pallas_call(
        _mm_swiglu_kernel,
        grid=(m // tm, nb),
        in_specs=[pl.BlockSpec((tm, k), lambda i, j: (i, 0)),
                  _wspec((k, tn), lambda i, j: (0, j), layer),
                  _wspec((k, tn), lambda i, j: (0, nb + j), layer)],
        out_specs=pl.BlockSpec((tm, tn), lambda i, j: (i, j)),
        out_shape=jax.ShapeDtypeStruct((m, f), BF16),
        compiler_params=_params("parallel", "arbitrary"),
        name=name,
    )(x, w, w)


def _mm_resid_kernel(x_ref, w_ref, r_ref, g_ref, o_ref, *, scale):
    acc = _bdot(x_ref[...], w_ref[...])
    o_ref[...] = r_ref[...] + (scale * g_ref[0]) * acc


def _mm_resid(x, w, res, gate, layer, *, scale, tm, tn, rows_per_gate, name, single_buffer_x=False):
    m, k = x.shape
    n = w.shape[2]
    r = gate.shape[1]
    if r == 1:
        assert rows_per_gate % tm == 0
        gmap = lambda i, j: ((i * tm) // rows_per_gate, 0, j)
    else:
        assert r == tm
        gmap = lambda i, j: (i, 0, j)
    x_mode = dict(pipeline_mode=pl.Buffered(1)) if single_buffer_x else {}
    return pl.pallas_call(
        functools.partial(_mm_resid_kernel, scale=scale),
        grid=(m // tm, n // tn),
        in_specs=[pl.BlockSpec((tm, k), lambda i, j: (i, 0), **x_mode),
                  _wspec((k, tn), lambda i, j: (0, j), layer),
                  pl.BlockSpec((tm, tn), lambda i, j: (i, j)),
                  pl.BlockSpec((1, r, tn), gmap)],
        out_specs=pl.BlockSpec((tm, tn), lambda i, j: (i, j)),
        out_shape=jax.ShapeDtypeStruct((m, n), F32),
        compiler_params=_params("parallel", "arbitrary"),
        name=name,
    )(x, w, res, gate)


def _mm_merge_kernel(oa_ref, wa_ref, os_ref, ws_ref, ga_ref, gs_ref, o_ref):
    a = _bdot(oa_ref[...], wa_ref[...])
    s = _bdot(os_ref[...], ws_ref[...])
    o_ref[...] = (ga_ref[...].astype(F32) * a + gs_ref[...].astype(F32) * s).astype(o_ref.dtype)


def _mm_merge(oa, wa, os_, ws, gates, layer, *, tm, tn, name):
    m = oa.shape[0]
    n = wa.shape[2]
    nb = n // tn
    return pl.pallas_call(
        _mm_merge_kernel,
        grid=(m // tm, nb),
        in_specs=[pl.BlockSpec((tm, oa.shape[1]), lambda i, j: (i, 0)),
                  _wspec((wa.shape[1], tn), lambda i, j: (0, j), layer),
                  pl.BlockSpec((tm, os_.shape[1]), lambda i, j: (i, 0)),
                  _wspec((ws.shape[1], tn), lambda i, j: (0, j), layer),
                  pl.BlockSpec((tm, tn), lambda i, j: (i, j)),
                  pl.BlockSpec((tm, tn), lambda i, j: (i, nb + j))],
        out_specs=pl.BlockSpec((tm, tn), lambda i, j: (i, j)),
        out_shape=jax.ShapeDtypeStruct((m, n), BF16),
        compiler_params=_params("parallel", "arbitrary"),
        name=name,
    )(oa, wa, os_, ws, gates, gates)


def _modnorm_kernel(x_ref, g_ref, sh_ref, sc_ref, o_ref):
    x = x_ref[...]
    y = x * lax.rsqrt(jnp.mean(x * x, axis=-1, keepdims=True) + EPS)
    y = y * g_ref[...]
    o_ref[...] = (y * (1.0 + sc_ref[0]) + sh_ref[0]).astype(o_ref.dtype)


def _modnorm(x, g, shift, scale, *, rows_per_seq, tm, name):
    m, d = x.shape
    r = shift.shape[1]
    if r == 1:
        assert rows_per_seq % tm == 0
        smap = lambda i: ((i * tm) // rows_per_seq, 0, 0)
    else:
        assert r == tm
        smap = lambda i: (i, 0, 0)
    return pl.pallas_call(
        _modnorm_kernel,
        grid=(m // tm,),
        in_specs=[pl.BlockSpec((tm, d), lambda i: (i, 0)),
                  pl.BlockSpec((1, d), lambda i: (0, 0)),
                  pl.BlockSpec((1, r, d), smap),
                  pl.BlockSpec((1, r, d), smap)],
        out_specs=pl.BlockSpec((tm, d), lambda i: (i, 0)),
        out_shape=jax.ShapeDtypeStruct((m, d), BF16),
        compiler_params=_params("parallel"),
        name=name,
    )(x, g, shift, scale)


def _rmsnorm_kernel(x_ref, g_ref, o_ref):
    x = x_ref[...]
    y = x * lax.rsqrt(jnp.mean(x * x, axis=-1, keepdims=True) + EPS)
    o_ref[...] = y * g_ref[...]


def _rmsnorm(x, g, *, tm, name):
    m, d = x.shape
    return pl.pallas_call(
        _rmsnorm_kernel,
        grid=(m // tm,),
        in_specs=[pl.BlockSpec((tm, d), lambda i: (i, 0)),
                  pl.BlockSpec((1, d), lambda i: (0, 0))],
        out_specs=pl.BlockSpec((tm, d), lambda i: (i, 0)),
        out_shape=jax.ShapeDtypeStruct((m, d), F32),
        compiler_params=_params("parallel"),
        name=name,
    )(x, g)


def _cumsum_rhs():
    j = lax.broadcasted_iota(jnp.int32, (2 * LANES, 2 * LANES), 0) & (LANES - 1)
    s = lax.broadcasted_iota(jnp.int32, (2 * LANES, 2 * LANES), 1)
    return jnp.where((j > s) | (s >= LANES), 1.0, 0.0).astype(BF16)


def _sb_weights(z_tiles, vis_tiles, carry, rhs):
    ls_t, part_t = [], []
    for z, vis in zip(z_tiles, vis_tiles):
        t = jnp.log(1.0 + jnp.exp(-jnp.abs(z)))
        ls_t.append(jnp.minimum(z, 0.0) - t)
        lk = -jnp.maximum(z, 0.0) - t
        if vis is not None:
            lk = jnp.where(vis, lk, 0.0)
        hi = lk.astype(BF16)
        lo = (lk - hi.astype(F32)).astype(BF16)
        part_t.append(jnp.dot(jnp.concatenate([hi, lo], axis=1), rhs, preferred_element_type=F32))
    w_t = [None] * len(z_tiles)
    for i in range(len(z_tiles) - 1, -1, -1):
        later = carry + part_t[i][:, :LANES]
        w = jnp.exp(ls_t[i] + later)
        if vis_tiles[i] is not None:
            w = jnp.where(vis_tiles[i], w, 0.0)
        w_t[i] = w
        carry = carry + part_t[i][:, LANES:]
    return w_t, carry


def _attn_prompt_kernel(bias_ref, q_ref, k_ref, v_ref, o_ref, acc_ref, carry_ref):
    hg = pl.program_id(1)
    qi = pl.program_id(2)
    nh = ATTN_HEADS_PER_STEP
    nt = ATTN_BLOCK // LANES
    rhs = _cumsum_rhs()
    row = lax.broadcasted_iota(jnp.int32, (ATTN_BLOCK, LANES), 0)
    col = lax.broadcasted_iota(jnp.int32, (ATTN_BLOCK, LANES), 1)
    diag_vis = [col + t * LANES < row for t in range(nt)]

    def hcols(h):
        return slice(h * SB_HEAD_DIM, (h + 1) * SB_HEAD_DIM)

    def block(kb, vis_tiles, first):
        start = pl.multiple_of(kb * ATTN_BLOCK, ATTN_BLOCK)
        for h in range(nh):
            bias = bias_ref[hg * nh + h]
            q = q_ref[0, :, hcols(h)]
            kblk = k_ref[0, pl.ds(start, ATTN_BLOCK), hcols(h)]
            vblk = v_ref[0, pl.ds(start, ATTN_BLOCK), hcols(h)]
            z = _bdot_nt(q, kblk) + bias
            carry = jnp.zeros((ATTN_BLOCK, LANES), F32) if first else carry_ref[h]
            w_t, carry = _sb_weights([z[:, t * LANES:(t + 1) * LANES] for t in range(nt)], vis_tiles, carry, rhs)
            carry_ref[h] = carry
            pv = jnp.dot(jnp.concatenate([w.astype(BF16) for w in w_t], axis=1), vblk, preferred_element_type=F32)
            acc_ref[h] = pv if first else acc_ref[h] + pv

    block(qi, diag_vis, True)

    def body(it, _):
        block(qi - 1 - it, [None] * nt, False)
        return 0

    lax.fori_loop(0, qi, body, 0)
    for h in range(nh):
        o_ref[0, :, hcols(h)] = acc_ref[h].astype(o_ref.dtype)


def _attn_prompt(qkv, sb_bias):
    b, t, _ = qkv.shape
    nq = t // ATTN_BLOCK
    nh = ATTN_HEADS_PER_STEP
    w = nh * SB_HEAD_DIM
    ng = SB_HEADS // nh
    return pl.pallas_call(
        _attn_prompt_kernel,
        grid=(b, ng, nq),
        in_specs=[pl.BlockSpec(memory_space=pltpu.SMEM),
                  pl.BlockSpec((1, ATTN_BLOCK, w), lambda bi, h, qi: (bi, qi, h)),
                  pl.BlockSpec((1, t, w), lambda bi, h, qi: (bi, 0, ng + h)),
                  pl.BlockSpec((1, t, w), lambda bi, h, qi: (bi, 0, 2 * ng + h))],
        out_specs=pl.BlockSpec((1, ATTN_BLOCK, w), lambda bi, h, qi: (bi, qi, h)),
        out_shape=jax.ShapeDtypeStruct((b, t, D_ATTN), BF16),
        scratch_shapes=[pltpu.VMEM((nh, ATTN_BLOCK, SB_HEAD_DIM), F32),
                        pltpu.VMEM((nh, ATTN_BLOCK, LANES), F32)],
        compiler_params=_params("parallel", "parallel", "arbitrary"),
        name="attn_prompt",
    )(sb_bias, qkv, qkv, qkv)


def _attn_sample_kernel(pt_ref, bias_ref, q_ref, kn_ref, vn_ref, *refs, tq):
    npg = PAGES_PER_STEP
    kc_refs, vc_refs = refs[:npg], refs[npg:2 * npg]
    o_ref, acc_ref, carry_ref, spread_ref, own_ref = refs[2 * npg:]
    s = pl.program_id(1)
    rows = SB_HEADS * tq
    page_rows = CHUNK * SB_HEADS
    rhs = _cumsum_rhs()

    def hcols(h):
        return slice(h * SB_HEAD_DIM, (h + 1) * SB_HEAD_DIM)

    def process(k_tiles, pv, vis):
        z_t = [jnp.concatenate([_bdot_nt(q_ref[:, hcols(h)], k_of(h)) + bias_ref[h] for h in range(SB_HEADS)], axis=0)
               for k_of in k_tiles]
        w_t, carry = _sb_weights(z_t, [vis] * len(z_t), carry_ref[...], rhs)
        carry_ref[...] = carry
        pv(w_t)

    @pl.when(s == 0)
    def _():
        carry_ref[...] = jnp.zeros_like(carry_ref)
        acc_ref[...] = jnp.zeros_like(acc_ref)
        tok = lax.broadcasted_iota(jnp.int32, (rows, page_rows), 0)
        r_i = lax.broadcasted_iota(jnp.int32, (rows, page_rows), 1)
        spread_ref[...] = jnp.where(lax.shift_right_logical(r_i, HEADS_SHIFT) == tok, 1.0, 0.0).astype(BF16)
        own_ref[...] = jnp.where((r_i & (SB_HEADS - 1)) == lax.shift_right_logical(tok, TQ_SHIFT), 1.0, 0.0).astype(BF16)

        pad = jnp.zeros((CHUNK - tq, SB_HEAD_DIM), F32)
        r_idx = lax.broadcasted_iota(jnp.int32, (rows, CHUNK), 0) & (tq - 1)
        s_idx = lax.broadcasted_iota(jnp.int32, (rows, CHUNK), 1)

        def pv_new(w_t):
            for h in range(SB_HEADS):
                v_h = jnp.concatenate([vn_ref[:, hcols(h)], pad], axis=0)
                acc_ref[h * tq:(h + 1) * tq, :] += _bdot(w_t[0][h * tq:(h + 1) * tq, :], v_h)

        process([lambda h: jnp.concatenate([kn_ref[:, hcols(h)], pad], axis=0)], pv_new, s_idx < r_idx)

    def head_rows(r, h):
        return r[pl.ds(h, CHUNK, stride=SB_HEADS), :]

    def pv_pages(w_t):
        upd = jnp.zeros((rows, SB_HEAD_DIM), F32)
        for w, v_ref in zip(w_t, reversed(vc_refs)):
            wide = jnp.dot(w.astype(BF16), spread_ref[...], preferred_element_type=F32)
            upd = upd + jnp.dot(wide.astype(BF16) * own_ref[...], v_ref[...].astype(BF16),
                                preferred_element_type=F32)
        acc_ref[...] += upd

    process([functools.partial(head_rows, r) for r in reversed(kc_refs)], pv_pages, None)

    @pl.when(s == pl.num_programs(1) - 1)
    def _():
        for h in range(SB_HEADS):
            o_ref[:, hcols(h)] = acc_ref[h * tq:(h + 1) * tq, :].astype(o_ref.dtype)


def _attn_sample(qkv, cache_k, cache_v, page_table, sb_bias, layer, *, tq):
    bs, n_pages = page_table.shape
    page_rows = cache_k.shape[2]
    npg = PAGES_PER_STEP
    assert page_rows == CHUNK * SB_HEADS and tq == SUBLANES == 1 << TQ_SHIFT and n_pages % npg == 0
    new_specs = [pl.BlockSpec((tq, D_ATTN), functools.partial(lambda sec, b, s, pt: (b, sec), sec))
                 for sec in range(3)]

    def page_spec(i):
        return pl.BlockSpec((None, None, page_rows, SB_HEAD_DIM),
                            lambda b, s, pt: (layer, pt[b, n_pages - 1 - (s * npg + i)], 0, 0))

    return pl.pallas_call(
        functools.partial(_attn_sample_kernel, tq=tq),
        grid_spec=pltpu.PrefetchScalarGridSpec(
            num_scalar_prefetch=1,
            grid=(bs, n_pages // npg),
            in_specs=[pl.BlockSpec(memory_space=pltpu.SMEM)] + new_specs + [page_spec(i) for i in range(npg)] * 2,
            out_specs=pl.BlockSpec((tq, D_ATTN), lambda b, s, pt: (b, 0)),
            scratch_shapes=[pltpu.VMEM((SB_HEADS * tq, SB_HEAD_DIM), F32),
                            pltpu.VMEM((SB_HEADS * tq, LANES), F32),
                            pltpu.VMEM((SB_HEADS * tq, page_rows), BF16),
                            pltpu.VMEM((SB_HEADS * tq, page_rows), BF16)],
        ),
        out_shape=jax.ShapeDtypeStruct((bs * tq, D_ATTN), F32),
        compiler_params=_params("parallel", "arbitrary"),
        name="attn_sample",
    )(page_table, sb_bias, qkv, qkv, qkv, *([cache_k] * npg), *([cache_v] * npg))


def _conv_kernel(cur_ref, prev_ref, buf_ref, w_ref, b_ref, o_ref, xp_ref):
    c = pl.program_id(1)
    rows = cur_ref.shape[1]
    prev = jnp.where(c == 0, buf_ref[0], prev_ref[0])
    xp_ref[0:SUBLANES, :] = prev
    xp_ref[SUBLANES:, :] = cur_ref[0]
    out = b_ref[...] + xp_ref[pl.ds(SUBLANES - CONV_WIDTH + 1, rows), :] * w_ref[0:1, :]
    for i in range(1, CONV_WIDTH):
        out = out + xp_ref[pl.ds(SUBLANES - CONV_WIDTH + 1 + i, rows), :] * w_ref[i:i + 1, :]
    o_ref[0] = _silu(out)


def _conv(zx, buf8, conv_w, conv_b, *, col0, tc, name):
    b, t, _ = zx.shape
    ch = conv_w.shape[1]
    c0 = col0 // tc
    rows = min(t, CONV_ROWS)
    per = rows // SUBLANES
    return pl.pallas_call(
        _conv_kernel,
        grid=(b, t // rows, ch // tc),
        in_specs=[pl.BlockSpec((1, rows, tc), lambda bi, c, j: (bi, c, c0 + j)),
                  pl.BlockSpec((1, SUBLANES, tc), lambda bi, c, j: (bi, jnp.maximum(c * per - 1, 0), c0 + j)),
                  pl.BlockSpec((1, SUBLANES, tc), lambda bi, c, j: (bi, 0, j)),
                  pl.BlockSpec((CONV_WIDTH, tc), lambda bi, c, j: (0, j)),
                  pl.BlockSpec((1, tc), lambda bi, c, j: (0, j))],
        out_specs=pl.BlockSpec((1, rows, tc), lambda bi, c, j: (bi, c, j)),
        out_shape=jax.ShapeDtypeStruct((b, t, ch), F32),
        scratch_shapes=[pltpu.VMEM((rows + SUBLANES, tc), F32)],
        compiler_params=_params("parallel", "parallel", "parallel"),
        name=name,
    )(zx, zx, buf8, conv_w, conv_b)


def _ssd_kernel(xs_ref, bm_ref, cm_ref, z_ref, dt_ref, dtt_ref, h0_ref, dtb_row_ref, dtb_col_ref, alog_row_ref,
                alog_col_ref, dskip_ref, ng_ref, y_ref, hout_ref, ht_ref, *, t_valid):
    c = pl.program_id(2)
    gps = bm_ref.shape[2] // SSM_STATE
    gw = xs_ref.shape[2] // gps
    hpg = gw // SSM_HEAD_DIM
    f32dot = functools.partial(jnp.dot, preferred_element_type=F32)

    @pl.when(c == 0)
    def _():
        ht_ref[...] = h0_ref[0].T

    row = lax.broadcasted_iota(jnp.int32, (CHUNK, CHUNK), 0)
    col = lax.broadcasted_iota(jnp.int32, (CHUNK, CHUNK), 1)
    causal = row >= col
    tri_incl = jnp.where(causal, 1.0, 0.0).astype(BF16)
    tri_incl_t = jnp.where(col >= row, 1.0, 0.0).astype(BF16)
    e_row = lax.broadcasted_iota(jnp.int32, (LANES, gw), 0)
    e_col = lax.broadcasted_iota(jnp.int32, (LANES, gw), 1)
    t_col = c * CHUNK + lax.broadcasted_iota(jnp.int32, (CHUNK, gw), 0)
    t_row = c * CHUNK + lax.broadcasted_iota(jnp.int32, (hpg, CHUNK), 1)
    lane = lax.broadcasted_iota(jnp.int32, (CHUNK, LANES), 1)
    first_head = lane < SSM_HEAD_DIM

    for u in range(gps):
        g = pl.program_id(1) * gps + u
        ch = slice(u * gw, (u + 1) * gw)
        st = slice(u * SSM_STATE, (u + 1) * SSM_STATE)

        expand = jnp.where(e_row == g * hpg + lax.shift_right_logical(e_col, HEAD_DIM_SHIFT), 1.0, 0.0).astype(BF16)

        dt_raw = _exact_dot(dt_ref[0], expand)
        dt = jnp.where(t_col < t_valid, _softplus(dt_raw + dtb_row_ref[:, ch]), 0.0)
        a_row = -jnp.exp(alog_row_ref[:, ch])
        hi, mid, lo = _split3(dt * a_row)
        a_cum = f32dot(tri_incl, hi) + f32dot(tri_incl, mid) + f32dot(tri_incl, lo)
        a_end = a_cum[CHUNK - 1:CHUNK, :]

        h_lo = pl.multiple_of(g * hpg, hpg)
        dt_r = jnp.where(t_row < t_valid,
                         _softplus(dtt_ref[0, pl.ds(h_lo, hpg), :] + dtb_col_ref[pl.ds(h_lo, hpg), :]), 0.0)
        hi, mid, lo = _split3(dt_r * (-jnp.exp(alog_col_ref[pl.ds(h_lo, hpg), :])))
        a_cum_r = f32dot(hi, tri_incl_t) + f32dot(mid, tri_incl_t) + f32dot(lo, tri_incl_t)

        xs = xs_ref[0, :, ch]
        bm = bm_ref[0, :, st]
        cm = cm_ref[0, :, st].astype(BF16)
        xdt = xs * dt
        xdt_b = xdt.astype(BF16)
        cb = _bdot_nt(cm, bm)

        ht = ht_ref[:, ch]
        y = f32dot(cm, ht.astype(BF16)) * jnp.exp(a_cum)
        y_diag = []
        for p in range(hpg // 2):
            cols = slice(p * LANES, (p + 1) * LANES)
            scores = []
            for j in (2 * p, 2 * p + 1):
                a_l = jnp.broadcast_to(a_cum[:, j * SSM_HEAD_DIM:j * SSM_HEAD_DIM + 1], (CHUNK, CHUNK))
                seg = a_l - a_cum_r[j:j + 1, :]
                scores.append((cb * jnp.exp(jnp.where(causal, seg, -jnp.inf))).astype(BF16))
            xp = xdt_b[:, cols]
            rhs = jnp.concatenate([jnp.where(first_head, xp, jnp.zeros_like(xp)),
                                   jnp.where(first_head, jnp.zeros_like(xp), xp)], axis=0)
            y_diag.append(f32dot(jnp.concatenate(scores, axis=1), rhs))
        y = y + jnp.concatenate(y_diag, axis=1)

        xd = (xdt * jnp.exp(a_end - a_cum)).astype(BF16)
        ht_ref[:, ch] = ht * jnp.exp(a_end) + f32dot(bm.T.astype(BF16), xd)

        y = y + xs * dskip_ref[:, ch]
        y = y * _silu(z_ref[0, :, ch])
        y = y * lax.rsqrt(jnp.mean(y * y, axis=-1, keepdims=True) + EPS)
        y_ref[0, :, ch] = (y * ng_ref[:, ch]).astype(y_ref.dtype)

    @pl.when(c == pl.num_programs(2) - 1)
    def _():
        hout_ref[0] = ht_ref[...].T


def _ssd(xc, zx, dt, dtt, h0, dtb_row, dtb_col, alog_row, alog_col, dskip, ng, *, t_valid, name):
    b, t, cc = xc.shape
    d_ssm = cc - 2 * SSM_GROUPS * SSM_STATE
    gps = SSD_GROUPS_PER_STEP
    gw = gps * d_ssm // SSM_GROUPS
    sw = gps * SSM_STATE
    nb0 = d_ssm // sw
    assert d_ssm % sw == 0 and SSM_GROUPS % gps == 0
    gmap = lambda bi, g, c: (0, g)
    return pl.pallas_call(
        functools.partial(_ssd_kernel, t_valid=t_valid),
        grid=(b, SSM_GROUPS // gps, t // CHUNK),
        in_specs=[pl.BlockSpec((1, CHUNK, gw), lambda bi, g, c: (bi, c, g)),
                  pl.BlockSpec((1, CHUNK, sw), lambda bi, g, c: (bi, c, nb0 + g)),
                  pl.BlockSpec((1, CHUNK, sw), lambda bi, g, c: (bi, c, nb0 + SSM_GROUPS // gps + g)),
                  pl.BlockSpec((1, CHUNK, gw), lambda bi, g, c: (bi, c, g)),
                  pl.BlockSpec((1, CHUNK, LANES), lambda bi, g, c: (bi, c, 0)),
                  pl.BlockSpec((1, LANES, CHUNK), lambda bi, g, c: (bi, 0, c)),
                  pl.BlockSpec((1, gw, SSM_STATE), lambda bi, g, c: (bi, g, 0)),
                  pl.BlockSpec((1, gw), gmap),
                  pl.BlockSpec((LANES, 1), lambda bi, g, c: (0, 0)),
                  pl.BlockSpec((1, gw), gmap),
                  pl.BlockSpec((LANES, 1), lambda bi, g, c: (0, 0)),
                  pl.BlockSpec((1, gw), gmap),
                  pl.BlockSpec((1, gw), gmap)],
        out_specs=[pl.BlockSpec((1, CHUNK, gw), lambda bi, g, c: (bi, c, g)),
                   pl.BlockSpec((1, gw, SSM_STATE), lambda bi, g, c: (bi, g, 0))],
        out_shape=[jax.ShapeDtypeStruct((b, t, d_ssm), BF16),
                   jax.ShapeDtypeStruct((b, d_ssm, SSM_STATE), F32)],
        scratch_shapes=[pltpu.VMEM((SSM_STATE, gw), F32)],
        compiler_params=_params("parallel", "parallel", "arbitrary"),
        name=name,
    )(xc, xc, xc, zx, dt, dtt, h0, dtb_row, dtb_col, alog_row, alog_col, dskip, ng)


def _pad_heads(v):
    return jnp.pad(v, (0, LANES - v.shape[0]))


def _layer(x, mod, attn_fn, h0, conv_buf, wts, small, layer, *, rows_per_seq, tm, tn_ffn, split_qkv, tag):
    m, d = x.shape
    nseq = m // rows_per_seq
    (w_ffn1_in, w_ffn1_out, w_in, w_gate, w_branch_attn, w_branch_ssm, w_out, w_ffn2_in, w_ffn2_out) = wts
    (norm_g, w_dt, conv_w, conv_b, dt_bias, a_log, d_skip, ssm_norm_g, b_gate) = small
    per_seq = rows_per_seq % tm == 0
    tnorm = 256 if per_seq else m
    big_ffn_out = tm * w_ffn1_out.shape[1] * 2 > 16 * 1024 * 1024
    d_ssm = w_branch_ssm.shape[1]
    conv_ch = conv_w.shape[1]
    rep = lambda v: jnp.repeat(v, SSM_HEAD_DIM)[None, :]
    nm = lambda s: f"{s}_{tag}"

    def mvec(idx):
        v = mod[:, idx:idx + 1, :]
        return v if per_seq else jnp.repeat(v[:, 0, :], rows_per_seq, axis=0)[None]

    def modnorm(x, k):
        return _modnorm(x, norm_g[k:k + 1], mvec(3 * k), mvec(3 * k + 1), rows_per_seq=rows_per_seq, tm=tnorm,
                        name=nm(f"modnorm{k}"))

    def ffn(x, k, w_a, w_b):
        hdn = _mm_swiglu(modnorm(x, k), w_a, layer, tm=tm, tn=tn_ffn, name=nm(f"ffn{k}_in"))
        return _mm_resid(hdn, w_b, x, mvec(3 * k + 2), layer, scale=0.5, tm=tm, tn=tn_ffn,
                         rows_per_gate=rows_per_seq, name=nm(f"ffn{k}_out"), single_buffer_x=big_ffn_out)

    x = ffn(x, 0, w_ffn1_in, w_ffn1_out)

    n = modnorm(x, 1)
    zero_bias = jnp.zeros((1, w_in.shape[2]), F32)
    qkv = _mm_qkv(n, w_in, layer, tm=tm, tn=512, scale=SB_HEAD_DIM ** -0.5, split=split_qkv, name=nm("proj_qkv"))
    zx = _mm_bias(n, w_in, zero_bias, layer, col0=3 * D_ATTN, n=d_ssm + conv_ch, tm=tm, tn=512, name=nm("proj_zx"))
    dt_raw = _mm_bias(n, w_dt, zero_bias[:, :LANES], 0, col0=0, n=LANES, tm=tm, tn=LANES, name=nm("proj_dt"))
    gates = _mm_sigmoid(n, w_gate, b_gate[None, :], layer, tm=tm, tn=512, name=nm("gates"))

    o_attn = attn_fn(qkv[0] if split_qkv else qkv)
    kv = qkv[1] if split_qkv else qkv[:, D_ATTN:]

    t_pad = -(-rows_per_seq // CHUNK) * CHUNK
    zx3 = zx.reshape(nseq, rows_per_seq, d_ssm + conv_ch)
    dt3 = dt_raw.reshape(nseq, rows_per_seq, LANES)
    if t_pad != rows_per_seq:
        padt = ((0, 0), (0, t_pad - rows_per_seq), (0, 0))
        zx3, dt3 = jnp.pad(zx3, padt), jnp.pad(dt3, padt)
    buf8 = jnp.pad(conv_buf, ((0, 0), (SUBLANES - conv_buf.shape[1], 0), (0, 0)))
    xc = _conv(zx3, buf8, conv_w, conv_b[None, :], col0=d_ssm, tc=1024, name=nm("conv"))
    y_ssm, h_new = _ssd(xc, zx3, dt3, jnp.swapaxes(dt3, 1, 2), h0,
                        rep(dt_bias), _pad_heads(dt_bias)[:, None], rep(a_log), _pad_heads(a_log)[:, None],
                        rep(d_skip), ssm_norm_g[None, :], t_valid=rows_per_seq, name=nm("ssd"))
    o_ssm = y_ssm[:, :rows_per_seq].reshape(m, d_ssm)

    merged = _mm_merge(o_attn, w_branch_attn, o_ssm, w_branch_ssm, gates, layer, tm=tm, tn=256, name=nm("merge"))
    x = _mm_resid(merged, w_out, x, mvec(5), layer, scale=1.0, tm=tm, tn=512, rows_per_gate=rows_per_seq,
                  name=nm("mix_out"))

    x = ffn(x, 2, w_ffn2_in, w_ffn2_out)
    return x, kv, zx3, h_new


def kernel(x_prompt, x_sample, c_prompt, c_sample, cache_k, cache_v, page_table, state_ssm, state_conv, norm_g,
           w_mod, b_mod, w_ffn1_in, w_ffn1_out, w_in, sb_bias, conv_w, conv_b, dt_bias, a_log, d_skip, ssm_norm_g,
           w_gate, b_gate, w_branch_attn, w_branch_ssm, w_out, w_ffn2_in, w_ffn2_out, final_norm_g):
    bp, tp, d = x_prompt.shape
    bs, ts, _ = x_sample.shape
    depth = norm_g.shape[0]
    n_pool, page = cache_k.shape[1], cache_k.shape[2]
    d_ssm = w_branch_ssm.shape[1]
    n_heads = dt_bias.shape[1]
    conv_ch = conv_w.shape[2]
    n_main = 3 * D_ATTN + d_ssm + conv_ch

    xp = x_prompt.reshape(bp * tp, d)
    xs = x_sample.reshape(bs * ts, d)
    c_all = jnp.concatenate([c_prompt, c_sample], axis=0)
    c_rows = -(-c_all.shape[0] // SUBLANES) * SUBLANES
    c_all = jnp.pad(c_all, ((0, c_rows - c_all.shape[0]), (0, 0)))
    h0_prompt = jnp.zeros((bp, d_ssm, SSM_STATE), F32)
    conv0_prompt = jnp.zeros((bp, CONV_WIDTH - 1, conv_ch), F32)
    ck = cache_k.reshape(depth, n_pool, page * SB_HEADS, SB_HEAD_DIM)
    cv = cache_v.reshape(depth, n_pool, page * SB_HEADS, SB_HEAD_DIM)
    wts = (w_ffn1_in, w_ffn1_out, w_in, w_gate, w_branch_attn, w_branch_ssm, w_out, w_ffn2_in, w_ffn2_out)

    outs = {k: [] for k in ("kp", "vp", "hp", "cp", "ks", "vs", "hs", "cs")}
    for l in range(depth):
        mod = _mm_bias(c_all, w_mod, b_mod[l][None, :], l, col0=0, n=N_MOD * d, tm=c_rows, tn=512,
                       pre_silu=True, name=f"mod_l{l}").reshape(c_rows, N_MOD, d)
        w_dt = jnp.pad(w_in[l, :, n_main:], ((0, 0), (0, LANES - n_heads)))[None]
        small = (norm_g[l], w_dt, conv_w[l], conv_b[l], dt_bias[l], a_log[l], d_skip[l], ssm_norm_g[l], b_gate[l])

        def attn_p(qkv):
            return _attn_prompt(qkv.reshape(bp, tp, 3 * D_ATTN), sb_bias[l]).reshape(bp * tp, D_ATTN)

        xp, kv_p, zx_p, h_p = _layer(xp, mod[:bp], attn_p, h0_prompt, conv0_prompt, wts, small, l,
                                     rows_per_seq=tp, tm=min(tp, 1024), tn_ffn=256, split_qkv=True, tag=f"p{l}")
        outs["kp"].append(kv_p[:, :D_ATTN].reshape(bp, tp, SB_HEADS, SB_HEAD_DIM))
        outs["vp"].append(kv_p[:, D_ATTN:].reshape(bp, tp, SB_HEADS, SB_HEAD_DIM))
        outs["hp"].append(h_p.reshape(bp, n_heads, SSM_HEAD_DIM, SSM_STATE))
        outs["cp"].append(zx_p[:, tp - (CONV_WIDTH - 1):, d_ssm:])

        def attn_s(qkv):
            return _attn_sample(qkv, ck, cv, page_table, sb_bias[l], l, tq=ts)

        xs, kv_s, zx_s, h_s = _layer(xs, mod[bp:bp + bs], attn_s, state_ssm[l].reshape(bs, d_ssm, SSM_STATE),
                                     state_conv[l], wts, small, l, rows_per_seq=ts, tm=bs * ts, tn_ffn=256,
                                     split_qkv=False, tag=f"s{l}")
        outs["ks"].append(kv_s[:, :D_ATTN].reshape(bs, ts, SB_HEADS, SB_HEAD_DIM))
        outs["vs"].append(kv_s[:, D_ATTN:].reshape(bs, ts, SB_HEADS, SB_HEAD_DIM))
        outs["hs"].append(h_s.reshape(bs, n_heads, SSM_HEAD_DIM, SSM_STATE))
        conv_in = jnp.concatenate([state_conv[l], zx_s[:, :ts, d_ssm:]], axis=1)
        outs["cs"].append(conv_in[:, -(CONV_WIDTH - 1):])

    y_prompt = _rmsnorm(xp, final_norm_g[None, :], tm=256, name="final_norm_p").reshape(bp, tp, d)
    y_sample = _rmsnorm(xs, final_norm_g[None, :], tm=bs * ts, name="final_norm_s").reshape(bs, ts, d)
    st = lambda k: jnp.stack(outs[k])
    return (y_prompt, y_sample, st("kp"), st("vp"), st("hp"), st("cp"), st("ks"), st("vs"), st("hs"), st("cs"))
```

```python
import functools

import jax
import jax.numpy as jnp
from jax import lax
from jax.experimental import pallas as pl
from jax.experimental.pallas import tpu as pltpu

F32 = jnp.float32
BF16 = jnp.bfloat16

SB_HEADS = 16
HEADS_SHIFT = 4
TQ_SHIFT = 3
SB_HEAD_DIM = 128
D_ATTN = SB_HEADS * SB_HEAD_DIM
SSM_HEAD_DIM = 64
HEAD_DIM_SHIFT = 6
SSM_GROUPS = 8
SSM_STATE = 128
CONV_WIDTH = 4
N_MOD = 9
EPS = 1e-6
CHUNK = 128
LANES = 128
SUBLANES = 8
VMEM_LIMIT = 56 * 1024 * 1024

ATTN_BLOCK = 256
ATTN_HEADS_PER_STEP = 8
PAGES_PER_STEP = 4
CONV_ROWS = 512
SSD_GROUPS_PER_STEP = 4


def _params(*sem):
    return pltpu.CompilerParams(dimension_semantics=sem, vmem_limit_bytes=VMEM_LIMIT)


def _wspec(block, imap, layer):
    return pl.BlockSpec((None,) + block, lambda *a: (layer,) + imap(*a))


def _sigmoid(x):
    return 1.0 / (1.0 + jnp.exp(-x))


def _silu(x):
    return x * _sigmoid(x)


def _softplus(x):
    return jnp.maximum(x, 0.0) + jnp.log1p(jnp.exp(-jnp.abs(x)))


def _bdot(a, b):
    return jnp.dot(a.astype(BF16), b.astype(BF16), preferred_element_type=F32)


def _bdot_nt(a, b):
    return lax.dot_general(a.astype(BF16), b.astype(BF16), (((1,), (1,)), ((), ())),
                           preferred_element_type=F32)


def _split3(x):
    hi = x.astype(BF16)
    r1 = x - hi.astype(F32)
    mid = r1.astype(BF16)
    lo = (r1 - mid.astype(F32)).astype(BF16)
    return hi, mid, lo


def _exact_dot(x, m01):
    hi, mid, lo = _split3(x)
    d = functools.partial(jnp.dot, preferred_element_type=F32)
    return d(hi, m01) + d(mid, m01) + d(lo, m01)


def _mm_bias_kernel(x_ref, w_ref, b_ref, o_ref, *, pre_silu, w_is_nk):
    x = x_ref[...]
    if pre_silu:
        x = _silu(x.astype(F32))
    o_ref[...] = (_bdot_nt if w_is_nk else _bdot)(x, w_ref[...]) + b_ref[...]


def _mm_bias(x, w, b, layer, *, col0, n, tm, tn, pre_silu=False, w_is_nk=False, name):
    m, k = x.shape
    c0 = col0 // tn
    w_spec = (_wspec((tn, k), lambda i, j: (c0 + j, 0), layer) if w_is_nk else
              _wspec((k, tn), lambda i, j: (0, c0 + j), layer))
    return pl.pallas_call(
        functools.partial(_mm_bias_kernel, pre_silu=pre_silu, w_is_nk=w_is_nk),
        grid=(m // tm, n // tn),
        in_specs=[pl.BlockSpec((tm, k), lambda i, j: (i, 0)),
                  w_spec,
                  pl.BlockSpec((1, tn), lambda i, j: (0, c0 + j))],
        out_specs=pl.BlockSpec((tm, tn), lambda i, j: (i, j)),
        out_shape=jax.ShapeDtypeStruct((m, n), F32),
        compiler_params=_params("parallel", "arbitrary"),
        name=name,
    )(x, w, b)


def _mm_qkv_kernel(x_ref, w_ref, *o_refs, nq, scale):
    j = pl.program_id(1)
    acc = _bdot_nt(x_ref[...], w_ref[...])
    o_refs[0][...] = (acc * jnp.where(j < nq, scale, 1.0)).astype(o_refs[0].dtype)
    if len(o_refs) > 1:
        @pl.when((j >= nq) & (j < 2 * nq))
        def _():
            o_refs[1][...] = acc

        @pl.when(j >= 2 * nq)
        def _():
            o_refs[2][...] = acc


def _mm_qkv(x, w, layer, *, tm, tn, scale, split, name):
    m, k = x.shape
    nq = D_ATTN // tn
    out_specs = [pl.BlockSpec((tm, tn), lambda i, j: (i, j))]
    out_shape = [jax.ShapeDtypeStruct((m, 3 * D_ATTN), BF16 if split else F32)]
    if split:
        for sec in (1, 2):
            out_specs.append(pl.BlockSpec(
                (tm, tn), functools.partial(lambda sec, i, j: (i, jnp.clip(j - sec * nq, 0, nq - 1)), sec)))
            out_shape.append(jax.ShapeDtypeStruct((m, D_ATTN), F32))
    outs = pl.pallas_call(
        functools.partial(_mm_qkv_kernel, nq=nq, scale=scale),
        grid=(m // tm, 3 * nq),
        in_specs=[pl.BlockSpec((tm, k), lambda i, j: (i, 0)),
                  _wspec((tn, k), lambda i, j: (j, 0), layer)],
        out_specs=out_specs,
        out_shape=out_shape,
        compiler_params=_params("parallel", "arbitrary"),
        name=name,
    )(x, w)
    return outs if split else outs[0]


def _mm_sigmoid_kernel(x_ref, w_ref, b_ref, o_ref):
    acc = _bdot(x_ref[...], w_ref[...])
    o_ref[...] = _sigmoid(acc + b_ref[...]).astype(o_ref.dtype)


def _mm_sigmoid(x, w, b, layer, *, tm, tn, name):
    m, k = x.shape
    n = w.shape[2]
    return pl.pallas_call(
        _mm_sigmoid_kernel,
        grid=(m // tm, n // tn),
        in_specs=[pl.BlockSpec((tm, k), lambda i, j: (i, 0)),
                  _wspec((k, tn), lambda i, j: (0, j), layer),
                  pl.BlockSpec((1, tn), lambda i, j: (0, j))],
        out_specs=pl.BlockSpec((tm, tn), lambda i, j: (i, j)),
        out_shape=jax.ShapeDtypeStruct((m, n), BF16),
        compiler_params=_params("parallel", "arbitrary"),
        name=name,
    )(x, w, b)


def _mm_swiglu_kernel(x_ref, wg_ref, wu_ref, o_ref):
    x = x_ref[...]
    gate = _bdot(x, wg_ref[...])
    up = _bdot(x, wu_ref[...])
    o_ref[...] = (_silu(gate) * up).astype(o_ref.dtype)


def _mm_swiglu(x, w, layer, *, tm, tn, name):
    m, k = x.shape
    f = w.shape[2] // 2
    nb = f // tn
    return pl.pallas_call(
        _mm_swiglu_kernel,
        grid=(m // tm, nb),
        in_specs=[pl.BlockSpec((tm, k), lambda i, j: (i, 0)),
                  _wspec((k, tn), lambda i, j: (0, j), layer),
                  _wspec((k, tn), lambda i, j: (0, nb + j), layer)],
        out_specs=pl.BlockSpec((tm, tn), lambda i, j: (i, j)),
        out_shape=jax.ShapeDtypeStruct((m, f), BF16),
        compiler_params=_params("parallel", "arbitrary"),
        name=name,
    )(x, w, w)


def _mm_resid_kernel(x_ref, w_ref, r_ref, g_ref, o_ref, *, scale):
    acc = _bdot(x_ref[...], w_ref[...])
    o_ref[...] = r_ref[...] + (scale * g_ref[0]) * acc


def _mm_resid(x, w, res, gate, layer, *, scale, tm, tn, rows_per_gate, name, single_buffer_x=False):
    m, k = x.shape
    n = w.shape[2]
    r = gate.shape[1]
    if r == 1:
        assert rows_per_gate % tm == 0
        gmap = lambda i, j: ((i * tm) // rows_per_gate, 0, j)
    else:
        assert r == tm
        gmap = lambda i, j: (i, 0, j)
    x_mode = dict(pipeline_mode=pl.Buffered(1)) if single_buffer_x else {}
    return pl.pallas_call(
        functools.partial(_mm_resid_kernel, scale=scale),
        grid=(m // tm, n // tn),
        in_specs=[pl.BlockSpec((tm, k), lambda i, j: (i, 0), **x_mode),
                  _wspec((k, tn), lambda i, j: (0, j), layer),
                  pl.BlockSpec((tm, tn), lambda i, j: (i, j)),
                  pl.BlockSpec((1, r, tn), gmap)],
        out_specs=pl.BlockSpec((tm, tn), lambda i, j: (i, j)),
        out_shape=jax.ShapeDtypeStruct((m, n), F32),
        compiler_params=_params("parallel", "arbitrary"),
        name=name,
    )(x, w, res, gate)


def _mm_merge_kernel(oa_ref, wa_ref, os_ref, ws_ref, ga_ref, gs_ref, o_ref):
    a = _bdot(oa_ref[...], wa_ref[...])
    s = _bdot(os_ref[...], ws_ref[...])
    o_ref[...] = (ga_ref[...].astype(F32) * a + gs_ref[...].astype(F32) * s).astype(o_ref.dtype)


def _mm_merge(oa, wa, os_, ws, gates, layer, *, tm, tn, name):
    m = oa.shape[0]
    n = wa.shape[2]
    nb = n // tn
    return pl.pallas_call(
        _mm_merge_kernel,
        grid=(m // tm, nb),
        in_specs=[pl.BlockSpec((tm, oa.shape[1]), lambda i, j: (i, 0)),
                  _wspec((wa.shape[1], tn), lambda i, j: (0, j), layer),
                  pl.BlockSpec((tm, os_.shape[1]), lambda i, j: (i, 0)),
                  _wspec((ws.shape[1], tn), lambda i, j: (0, j), layer),
                  pl.BlockSpec((tm, tn), lambda i, j: (i, j)),
                  pl.BlockSpec((tm, tn), lambda i, j: (i, nb + j))],
        out_specs=pl.BlockSpec((tm, tn), lambda i, j: (i, j)),
        out_shape=jax.ShapeDtypeStruct((m, n), BF16),
        compiler_params=_params("parallel", "arbitrary"),
        name=name,
    )(oa, wa, os_, ws, gates, gates)


def _modnorm_kernel(x_ref, g_ref, sh_ref, sc_ref, o_ref):
    x = x_ref[...]
    y = x * lax.rsqrt(jnp.mean(x * x, axis=-1, keepdims=True) + EPS)
    y = y * g_ref[...]
    o_ref[...] = (y * (1.0 + sc_ref[0]) + sh_ref[0]).astype(o_ref.dtype)


def _modnorm(x, g, shift, scale, *, rows_per_seq, tm, name):
    m, d = x.shape
    r = shift.shape[1]
    if r == 1:
        assert rows_per_seq % tm == 0
        smap = lambda i: ((i * tm) // rows_per_seq, 0, 0)
    else:
        assert r == tm
        smap = lambda i: (i, 0, 0)
    return pl.pallas_call(
        _modnorm_kernel,
        grid=(m // tm,),
        in_specs=[pl.BlockSpec((tm, d), lambda i: (i, 0)),
                  pl.BlockSpec((1, d), lambda i: (0, 0)),
                  pl.BlockSpec((1, r, d), smap),
                  pl.BlockSpec((1, r, d), smap)],
        out_specs=pl.BlockSpec((tm, d), lambda i: (i, 0)),
        out_shape=jax.ShapeDtypeStruct((m, d), BF16),
        compiler_params=_params("parallel"),
        name=name,
    )(x, g, shift, scale)


def _rmsnorm_kernel(x_ref, g_ref, o_ref):
    x = x_ref[...]
    y = x * lax.rsqrt(jnp.mean(x * x, axis=-1, keepdims=True) + EPS)
    o_ref[...] = y * g_ref[...]


def _rmsnorm(x, g, *, tm, name):
    m, d = x.shape
    return pl.pallas_call(
        _rmsnorm_kernel,
        grid=(m // tm,),
        in_specs=[pl.BlockSpec((tm, d), lambda i: (i, 0)),
                  pl.BlockSpec((1, d), lambda i: (0, 0))],
        out_specs=pl.BlockSpec((tm, d), lambda i: (i, 0)),
        out_shape=jax.ShapeDtypeStruct((m, d), F32),
        compiler_params=_params("parallel"),
        name=name,
    )(x, g)


def _cumsum_rhs():
    j = lax.broadcasted_iota(jnp.int32, (2 * LANES, 2 * LANES), 0) & (LANES - 1)
    s = lax.broadcasted_iota(jnp.int32, (2 * LANES, 2 * LANES), 1)
    return jnp.where((j > s) | (s >= LANES), 1.0, 0.0).astype(BF16)


def _sb_weights(z_tiles, vis_tiles, carry, rhs):
    ls_t, part_t = [], []
    for z, vis in zip(z_tiles, vis_tiles):
        t = jnp.log(1.0 + jnp.exp(-jnp.abs(z)))
        ls_t.append(jnp.minimum(z, 0.0) - t)
        lk = -jnp.maximum(z, 0.0) - t
        if vis is not None:
            lk = jnp.where(vis, lk, 0.0)
        hi = lk.astype(BF16)
        lo = (lk - hi.astype(F32)).astype(BF16)
        part_t.append(jnp.dot(jnp.concatenate([hi, lo], axis=1), rhs, preferred_element_type=F32))
    w_t = [None] * len(z_tiles)
    for i in range(len(z_tiles) - 1, -1, -1):
        later = carry + part_t[i][:, :LANES]
        w = jnp.exp(ls_t[i] + later)
        if vis_tiles[i] is not None:
            w = jnp.where(vis_tiles[i], w, 0.0)
        w_t[i] = w
        carry = carry + part_t[i][:, LANES:]
    return w_t, carry


def _attn_prompt_kernel(bias_ref, q_ref, k_ref, v_ref, o_ref, acc_ref, carry_ref):
    hg = pl.program_id(1)
    qi = pl.program_id(2)
    nh = ATTN_HEADS_PER_STEP
    nt = ATTN_BLOCK // LANES
    rhs = _cumsum_rhs()
    row = lax.broadcasted_iota(jnp.int32, (ATTN_BLOCK, LANES), 0)
    col = lax.broadcasted_iota(jnp.int32, (ATTN_BLOCK, LANES), 1)
    diag_vis = [col + t * LANES < row for t in range(nt)]

    def hcols(h):
        return slice(h * SB_HEAD_DIM, (h + 1) * SB_HEAD_DIM)

    def block(kb, vis_tiles, first):
        start = pl.multiple_of(kb * ATTN_BLOCK, ATTN_BLOCK)
        for h in range(nh):
            bias = bias_ref[hg * nh + h]
            q = q_ref[0, :, hcols(h)]
            kblk = k_ref[0, pl.ds(start, ATTN_BLOCK), hcols(h)]
            vblk = v_ref[0, pl.ds(start, ATTN_BLOCK), hcols(h)]
            z = _bdot_nt(q, kblk) + bias
            carry = jnp.zeros((ATTN_BLOCK, LANES), F32) if first else carry_ref[h]
            w_t, carry = _sb_weights([z[:, t * LANES:(t + 1) * LANES] for t in range(nt)], vis_tiles, carry, rhs)
            carry_ref[h] = carry
            pv = jnp.dot(jnp.concatenate([w.astype(BF16) for w in w_t], axis=1), vblk, preferred_element_type=F32)
            acc_ref[h] = pv if first else acc_ref[h] + pv

    block(qi, diag_vis, True)

    def body(it, _):
        block(qi - 1 - it, [None] * nt, False)
        return 0

    lax.fori_loop(0, qi, body, 0)
    for h in range(nh):
        o_ref[0, :, hcols(h)] = acc_ref[h].astype(o_ref.dtype)


def _attn_prompt(qkv, sb_bias):
    b, t, _ = qkv.shape
    nq = t // ATTN_BLOCK
    nh = ATTN_HEADS_PER_STEP
    w = nh * SB_HEAD_DIM
    ng = SB_HEADS // nh
    return pl.pallas_call(
        _attn_prompt_kernel,
        grid=(b, ng, nq),
        in_specs=[pl.BlockSpec(memory_space=pltpu.SMEM),
                  pl.BlockSpec((1, ATTN_BLOCK, w), lambda bi, h, qi: (bi, qi, h)),
                  pl.BlockSpec((1, t, w), lambda bi, h, qi: (bi, 0, ng + h)),
                  pl.BlockSpec((1, t, w), lambda bi, h, qi: (bi, 0, 2 * ng + h))],
        out_specs=pl.BlockSpec((1, ATTN_BLOCK, w), lambda bi, h, qi: (bi, qi, h)),
        out_shape=jax.ShapeDtypeStruct((b, t, D_ATTN), BF16),
        scratch_shapes=[pltpu.VMEM((nh, ATTN_BLOCK, SB_HEAD_DIM), F32),
                        pltpu.VMEM((nh, ATTN_BLOCK, LANES), F32)],
        compiler_params=_params("parallel", "parallel", "arbitrary"),
        name="attn_prompt",
    )(sb_bias, qkv, qkv, qkv)


def _attn_sample_kernel(pt_ref, bias_ref, q_ref, kn_ref, vn_ref, *refs, tq):
    npg = PAGES_PER_STEP
    kc_refs, vc_refs = refs[:npg], refs[npg:2 * npg]
    o_ref, acc_ref, carry_ref, spread_ref, own_ref = refs[2 * npg:]
    s = pl.program_id(1)
    rows = SB_HEADS * tq
    page_rows = CHUNK * SB_HEADS
    rhs = _cumsum_rhs()

    def hcols(h):
        return slice(h * SB_HEAD_DIM, (h + 1) * SB_HEAD_DIM)

    def process(k_tiles, pv, vis):
        z_t = [jnp.concatenate([_bdot_nt(q_ref[:, hcols(h)], k_of(h)) + bias_ref[h] for h in range(SB_HEADS)], axis=0)
               for k_of in k_tiles]
        w_t, carry = _sb_weights(z_t, [vis] * len(z_t), carry_ref[...], rhs)
        carry_ref[...] = carry
        pv(w_t)

    @pl.when(s == 0)
    def _():
        carry_ref[...] = jnp.zeros_like(carry_ref)
        acc_ref[...] = jnp.zeros_like(acc_ref)
        tok = lax.broadcasted_iota(jnp.int32, (rows, page_rows), 0)
        r_i = lax.broadcasted_iota(jnp.int32, (rows, page_rows), 1)
        spread_ref[...] = jnp.where(lax.shift_right_logical(r_i, HEADS_SHIFT) == tok, 1.0, 0.0).astype(BF16)
        own_ref[...] = jnp.where((r_i & (SB_HEADS - 1)) == lax.shift_right_logical(tok, TQ_SHIFT), 1.0, 0.0).astype(BF16)

        pad = jnp.zeros((CHUNK - tq, SB_HEAD_DIM), F32)
        r_idx = lax.broadcasted_iota(jnp.int32, (rows, CHUNK), 0) & (tq - 1)
        s_idx = lax.broadcasted_iota(jnp.int32, (rows, CHUNK), 1)

        def pv_new(w_t):
            for h in range(SB_HEADS):
                v_h = jnp.concatenate([vn_ref[:, hcols(h)], pad], axis=0)
                acc_ref[h * tq:(h + 1) * tq, :] += _bdot(w_t[0][h * tq:(h + 1) * tq, :], v_h)

        process([lambda h: jnp.concatenate([kn_ref[:, hcols(h)], pad], axis=0)], pv_new, s_idx < r_idx)

    def head_rows(r, h):
        return r[pl.ds(h, CHUNK, stride=SB_HEADS), :]

    def pv_pages(w_t):
        upd = jnp.zeros((rows, SB_HEAD_DIM), F32)
        for w, v_ref in zip(w_t, reversed(vc_refs)):
            wide = jnp.dot(w.astype(BF16), spread_ref[...], preferred_element_type=F32)
            upd = upd + jnp.dot(wide.astype(BF16) * own_ref[...], v_ref[...].astype(BF16),
                                preferred_element_type=F32)
        acc_ref[...] += upd

    process([functools.partial(head_rows, r) for r in reversed(kc_refs)], pv_pages, None)

    @pl.when(s == pl.num_programs(1) - 1)
    def _():
        for h in range(SB_HEADS):
            o_ref[:, hcols(h)] = acc_ref[h * tq:(h + 1) * tq, :].astype(o_ref.dtype)


def _attn_sample(qkv, cache_k, cache_v, page_table, sb_bias, layer, *, tq):
    bs, n_pages = page_table.shape
    page_rows = cache_k.shape[2]
    npg = PAGES_PER_STEP
    assert page_rows == CHUNK * SB_HEADS and tq == SUBLANES == 1 << TQ_SHIFT and n_pages % npg == 0
    new_specs = [pl.BlockSpec((tq, D_ATTN), functools.partial(lambda sec, b, s, pt: (b, sec), sec))
                 for sec in range(3)]

    def page_spec(i):
        return pl.BlockSpec((None, None, page_rows, SB_HEAD_DIM),
                            lambda b, s, pt: (layer, pt[b, n_pages - 1 - (s * npg + i)], 0, 0))

    return pl.pallas_call(
        functools.partial(_attn_sample_kernel, tq=tq),
        grid_spec=pltpu.PrefetchScalarGridSpec(
            num_scalar_prefetch=1,
            grid=(bs, n_pages // npg),
            in_specs=[pl.BlockSpec(memory_space=pltpu.SMEM)] + new_specs + [page_spec(i) for i in range(npg)] * 2,
            out_specs=pl.BlockSpec((tq, D_ATTN), lambda b, s, pt: (b, 0)),
            scratch_shapes=[pltpu.VMEM((SB_HEADS * tq, SB_HEAD_DIM), F32),
                            pltpu.VMEM((SB_HEADS * tq, LANES), F32),
                            pltpu.VMEM((SB_HEADS * tq, page_rows), BF16),
                            pltpu.VMEM((SB_HEADS * tq, page_rows), BF16)],
        ),
        out_shape=jax.ShapeDtypeStruct((bs * tq, D_ATTN), F32),
        compiler_params=_params("parallel", "arbitrary"),
        name="attn_sample",
    )(page_table, sb_bias, qkv, qkv, qkv, *([cache_k] * npg), *([cache_v] * npg))


def _conv_kernel(cur_ref, prev_ref, buf_ref, w_ref, b_ref, o_ref, xp_ref):
    c = pl.program_id(1)
    rows = cur_ref.shape[1]
    prev = jnp.where(c == 0, buf_ref[0], prev_ref[0])
    xp_ref[0:SUBLANES, :] = prev
    xp_ref[SUBLANES:, :] = cur_ref[0]
    out = b_ref[...] + xp_ref[pl.ds(SUBLANES - CONV_WIDTH + 1, rows), :] * w_ref[0:1, :]
    for i in range(1, CONV_WIDTH):
        out = out + xp_ref[pl.ds(SUBLANES - CONV_WIDTH + 1 + i, rows), :] * w_ref[i:i + 1, :]
    o_ref[0] = _silu(out)


def _conv(zx, buf8, conv_w, conv_b, *, col0, tc, name):
    b, t, _ = zx.shape
    ch = conv_w.shape[1]
    c0 = col0 // tc
    rows = min(t, CONV_ROWS)
    per = rows // SUBLANES
    return pl.pallas_call(
        _conv_kernel,
        grid=(b, t // rows, ch // tc),
        in_specs=[pl.BlockSpec((1, rows, tc), lambda bi, c, j: (bi, c, c0 + j)),
                  pl.BlockSpec((1, SUBLANES, tc), lambda bi, c, j: (bi, jnp.maximum(c * per - 1, 0), c0 + j)),
                  pl.BlockSpec((1, SUBLANES, tc), lambda bi, c, j: (bi, 0, j)),
                  pl.BlockSpec((CONV_WIDTH, tc), lambda bi, c, j: (0, j)),
                  pl.BlockSpec((1, tc), lambda bi, c, j: (0, j))],
        out_specs=pl.BlockSpec((1, rows, tc), lambda bi, c, j: (bi, c, j)),
        out_shape=jax.ShapeDtypeStruct((b, t, ch), F32),
        scratch_shapes=[pltpu.VMEM((rows + SUBLANES, tc), F32)],
        compiler_params=_params("parallel", "parallel", "parallel"),
        name=name,
    )(zx, zx, buf8, conv_w, conv_b)


def _ssd_kernel(xs_ref, bm_ref, cm_ref, z_ref, dt_ref, dtt_ref, h0_ref, dtb_row_ref, dtb_col_ref, alog_row_ref,
                alog_col_ref, dskip_ref, ng_ref, y_ref, hout_ref, ht_ref, *, t_valid):
    c = pl.program_id(2)
    gps = bm_ref.shape[2] // SSM_STATE
    gw = xs_ref.shape[2] // gps
    hpg = gw // SSM_HEAD_DIM
    f32dot = functools.partial(jnp.dot, preferred_element_type=F32)

    @pl.when(c == 0)
    def _():
        ht_ref[...] = h0_ref[0].T

    row = lax.broadcasted_iota(jnp.int32, (CHUNK, CHUNK), 0)
    col = lax.broadcasted_iota(jnp.int32, (CHUNK, CHUNK), 1)
    causal = row >= col
    tri_incl = jnp.where(causal, 1.0, 0.0).astype(BF16)
    tri_incl_t = jnp.where(col >= row, 1.0, 0.0).astype(BF16)
    e_row = lax.broadcasted_iota(jnp.int32, (LANES, gw), 0)
    e_col = lax.broadcasted_iota(jnp.int32, (LANES, gw), 1)
    t_col = c * CHUNK + lax.broadcasted_iota(jnp.int32, (CHUNK, gw), 0)
    t_row = c * CHUNK + lax.broadcasted_iota(jnp.int32, (hpg, CHUNK), 1)
    lane = lax.broadcasted_iota(jnp.int32, (CHUNK, LANES), 1)
    first_head = lane < SSM_HEAD_DIM
    dt_heads = jnp.where(lane < SSM_GROUPS * hpg, dt_ref[0], 0.0)

    for u in range(gps):
        g = pl.program_id(1) * gps + u
        ch = slice(u * gw, (u + 1) * gw)
        st = slice(u * SSM_STATE, (u + 1) * SSM_STATE)

        expand = jnp.where(e_row == g * hpg + lax.shift_right_logical(e_col, HEAD_DIM_SHIFT), 1.0, 0.0).astype(BF16)

        dt_raw = _exact_dot(dt_heads, expand)
        dt = jnp.where(t_col < t_valid, _softplus(dt_raw + dtb_row_ref[:, ch]), 0.0)
        a_row = -jnp.exp(alog_row_ref[:, ch])
        hi, mid, lo = _split3(dt * a_row)
        a_cum = f32dot(tri_incl, hi) + f32dot(tri_incl, mid) + f32dot(tri_incl, lo)
        a_end = a_cum[CHUNK - 1:CHUNK, :]

        h_lo = pl.multiple_of(g * hpg, hpg)
        dt_r = jnp.where(t_row < t_valid,
                         _softplus(dtt_ref[0, pl.ds(h_lo, hpg), :] + dtb_col_ref[pl.ds(h_lo, hpg), :]), 0.0)
        hi, mid, lo = _split3(dt_r * (-jnp.exp(alog_col_ref[pl.ds(h_lo, hpg), :])))
        a_cum_r = f32dot(hi, tri_incl_t) + f32dot(mid, tri_incl_t) + f32dot(lo, tri_incl_t)

        xs = xs_ref[0, :, ch]
        bm = bm_ref[0, :, st]
        cm = cm_ref[0, :, st].astype(BF16)
        xdt = xs * dt
        xdt_b = xdt.astype(BF16)
        cb = _bdot_nt(cm, bm)

        ht = ht_ref[:, ch]
        y = f32dot(cm, ht.astype(BF16)) * jnp.exp(a_cum)
        y_diag = []
        for p in range(hpg // 2):
            cols = slice(p * LANES, (p + 1) * LANES)
            scores = []
            for j in (2 * p, 2 * p + 1):
                a_l = jnp.broadcast_to(a_cum[:, j * SSM_HEAD_DIM:j * SSM_HEAD_DIM + 1], (CHUNK, CHUNK))
                seg = a_l - a_cum_r[j:j + 1, :]
                scores.append((cb * jnp.exp(jnp.where(causal, seg, -jnp.inf))).astype(BF16))
            xp = xdt_b[:, cols]
            rhs = jnp.concatenate([jnp.where(first_head, xp, jnp.zeros_like(xp)),
                                   jnp.where(first_head, jnp.zeros_like(xp), xp)], axis=0)
            y_diag.append(f32dot(jnp.concatenate(scores, axis=1), rhs))
        y = y + jnp.concatenate(y_diag, axis=1)

        xd = (xdt * jnp.exp(a_end - a_cum)).astype(BF16)
        ht_ref[:, ch] = ht * jnp.exp(a_end) + f32dot(bm.T.astype(BF16), xd)

        y = y + xs * dskip_ref[:, ch]
        y = y * _silu(z_ref[0, :, ch])
        y = y * lax.rsqrt(jnp.mean(y * y, axis=-1, keepdims=True) + EPS)
        y_ref[0, :, ch] = (y * ng_ref[:, ch]).astype(y_ref.dtype)

    @pl.when(c == pl.num_programs(2) - 1)
    def _():
        hout_ref[0] = ht_ref[...].T


def _ssd(xc, zx, dt, dtt, h0, dtb_row, dtb_col, alog_row, alog_col, dskip, ng, *, t_valid, name):
    b, t, cc = xc.shape
    d_ssm = cc - 2 * SSM_GROUPS * SSM_STATE
    gps = SSD_GROUPS_PER_STEP
    gw = gps * d_ssm // SSM_GROUPS
    sw = gps * SSM_STATE
    nb0 = d_ssm // sw
    assert d_ssm % sw == 0 and SSM_GROUPS % gps == 0
    gmap = lambda bi, g, c: (0, g)
    return pl.pallas_call(
        functools.partial(_ssd_kernel, t_valid=t_valid),
        grid=(b, SSM_GROUPS // gps, t // CHUNK),
        in_specs=[pl.BlockSpec((1, CHUNK, gw), lambda bi, g, c: (bi, c, g)),
                  pl.BlockSpec((1, CHUNK, sw), lambda bi, g, c: (bi, c, nb0 + g)),
                  pl.BlockSpec((1, CHUNK, sw), lambda bi, g, c: (bi, c, nb0 + SSM_GROUPS // gps + g)),
                  pl.BlockSpec((1, CHUNK, gw), lambda bi, g, c: (bi, c, g)),
                  pl.BlockSpec((1, CHUNK, LANES), lambda bi, g, c: (bi, c, 0)),
                  pl.BlockSpec((1, LANES, CHUNK), lambda bi, g, c: (bi, 0, c)),
                  pl.BlockSpec((1, gw, SSM_STATE), lambda bi, g, c: (bi, g, 0)),
                  pl.BlockSpec((1, gw), gmap),
                  pl.BlockSpec((LANES, 1), lambda bi, g, c: (0, 0)),
                  pl.BlockSpec((1, gw), gmap),
                  pl.BlockSpec((LANES, 1), lambda bi, g, c: (0, 0)),
                  pl.BlockSpec((1, gw), gmap),
                  pl.BlockSpec((1, gw), gmap)],
        out_specs=[pl.BlockSpec((1, CHUNK, gw), lambda bi, g, c: (bi, c, g)),
                   pl.BlockSpec((1, gw, SSM_STATE), lambda bi, g, c: (bi, g, 0))],
        out_shape=[jax.ShapeDtypeStruct((b, t, d_ssm), BF16),
                   jax.ShapeDtypeStruct((b, d_ssm, SSM_STATE), F32)],
        scratch_shapes=[pltpu.VMEM((SSM_STATE, gw), F32)],
        compiler_params=_params("parallel", "parallel", "arbitrary"),
        name=name,
    )(xc, xc, xc, zx, dt, dtt, h0, dtb_row, dtb_col, alog_row, alog_col, dskip, ng)


def _pad_heads(v):
    return jnp.pad(v, (0, LANES - v.shape[0]))


def _layer(x, mod, attn_fn, h0, conv_buf, wts, small, layer, *, rows_per_seq, tm, tn_ffn, split_qkv, tag):
    m, d = x.shape
    nseq = m // rows_per_seq
    (w_ffn1_in, w_ffn1_out, w_in_t, w_gate, w_branch_attn, w_branch_ssm, w_out, w_ffn2_in, w_ffn2_out) = wts
    (norm_g, conv_w, conv_b, dt_bias, a_log, d_skip, ssm_norm_g, b_gate) = small
    per_seq = rows_per_seq % tm == 0
    tnorm = 256 if per_seq else m
    big_ffn_out = tm * w_ffn1_out.shape[1] * 2 > 16 * 1024 * 1024
    d_ssm = w_branch_ssm.shape[1]
    conv_ch = conv_w.shape[1]
    rep = lambda v: jnp.repeat(v, SSM_HEAD_DIM)[None, :]
    nm = lambda s: f"{s}_{tag}"

    def mvec(idx):
        v = mod[:, idx:idx + 1, :]
        return v if per_seq else jnp.repeat(v[:, 0, :], rows_per_seq, axis=0)[None]

    def modnorm(x, k):
        return _modnorm(x, norm_g[k:k + 1], mvec(3 * k), mvec(3 * k + 1), rows_per_seq=rows_per_seq, tm=tnorm,
                        name=nm(f"modnorm{k}"))

    def ffn(x, k, w_a, w_b):
        hdn = _mm_swiglu(modnorm(x, k), w_a, layer, tm=tm, tn=tn_ffn, name=nm(f"ffn{k}_in"))
        return _mm_resid(hdn, w_b, x, mvec(3 * k + 2), layer, scale=0.5, tm=tm, tn=tn_ffn,
                         rows_per_gate=rows_per_seq, name=nm(f"ffn{k}_out"), single_buffer_x=big_ffn_out)

    x = ffn(x, 0, w_ffn1_in, w_ffn1_out)

    n = modnorm(x, 1)
    zero_bias = jnp.zeros((1, w_in_t.shape[1]), F32)
    qkv = _mm_qkv(n, w_in_t, layer, tm=tm, tn=512, scale=SB_HEAD_DIM ** -0.5, split=split_qkv, name=nm("proj_qkv"))
    zx = _mm_bias(n, w_in_t, zero_bias, layer, col0=3 * D_ATTN, n=d_ssm + conv_ch, tm=tm, tn=512, w_is_nk=True,
                  name=nm("proj_zx"))
    dt_raw = _mm_bias(n, w_in_t, zero_bias, layer, col0=3 * D_ATTN + d_ssm + conv_ch, n=LANES, tm=tm, tn=LANES,
                      w_is_nk=True, name=nm("proj_dt"))
    gates = _mm_sigmoid(n, w_gate, b_gate[None, :], layer, tm=tm, tn=512, name=nm("gates"))

    o_attn = attn_fn(qkv[0] if split_qkv else qkv)
    kv = qkv[1:] if split_qkv else (qkv[:, D_ATTN:2 * D_ATTN], qkv[:, 2 * D_ATTN:])

    t_pad = -(-rows_per_seq // CHUNK) * CHUNK
    zx3 = zx.reshape(nseq, rows_per_seq, d_ssm + conv_ch)
    dt3 = dt_raw.reshape(nseq, rows_per_seq, LANES)
    if t_pad != rows_per_seq:
        padt = ((0, 0), (0, t_pad - rows_per_seq), (0, 0))
        zx3, dt3 = jnp.pad(zx3, padt), jnp.pad(dt3, padt)
    buf8 = jnp.pad(conv_buf, ((0, 0), (SUBLANES - conv_buf.shape[1], 0), (0, 0)))
    xc = _conv(zx3, buf8, conv_w, conv_b[None, :], col0=d_ssm, tc=1024, name=nm("conv"))
    y_ssm, h_new = _ssd(xc, zx3, dt3, jnp.swapaxes(dt3, 1, 2), h0,
                        rep(dt_bias), _pad_heads(dt_bias)[:, None], rep(a_log), _pad_heads(a_log)[:, None],
                        rep(d_skip), ssm_norm_g[None, :], t_valid=rows_per_seq, name=nm("ssd"))
    o_ssm = y_ssm[:, :rows_per_seq].reshape(m, d_ssm)

    merged = _mm_merge(o_attn, w_branch_attn, o_ssm, w_branch_ssm, gates, layer, tm=tm, tn=256, name=nm("merge"))
    x = _mm_resid(merged, w_out, x, mvec(5), layer, scale=1.0, tm=tm, tn=512, rows_per_gate=rows_per_seq,
                  name=nm("mix_out"))

    x = ffn(x, 2, w_ffn2_in, w_ffn2_out)
    return x, kv, zx3, h_new


def kernel(x_prompt, x_sample, c_prompt, c_sample, cache_k, cache_v, page_table, state_ssm, state_conv, norm_g,
           w_mod, b_mod, w_ffn1_in, w_ffn1_out, w_in, sb_bias, conv_w, conv_b, dt_bias, a_log, d_skip, ssm_norm_g,
           w_gate, b_gate, w_branch_attn, w_branch_ssm, w_out, w_ffn2_in, w_ffn2_out, final_norm_g):
    bp, tp, d = x_prompt.shape
    bs, ts, _ = x_sample.shape
    depth = norm_g.shape[0]
    n_pool, page = cache_k.shape[1], cache_k.shape[2]
    d_ssm = w_branch_ssm.shape[1]
    n_heads = dt_bias.shape[1]
    conv_ch = conv_w.shape[2]
    n_main = 3 * D_ATTN + d_ssm + conv_ch

    xp = x_prompt.reshape(bp * tp, d)
    xs = x_sample.reshape(bs * ts, d)
    c_all = jnp.concatenate([c_prompt, c_sample], axis=0)
    c_rows = -(-c_all.shape[0] // SUBLANES) * SUBLANES
    c_all = jnp.pad(c_all, ((0, c_rows - c_all.shape[0]), (0, 0)))
    h0_prompt = jnp.zeros((bp, d_ssm, SSM_STATE), F32)
    conv0_prompt = jnp.zeros((bp, CONV_WIDTH - 1, conv_ch), F32)
    ck = cache_k.reshape(depth, n_pool, page * SB_HEADS, SB_HEAD_DIM)
    cv = cache_v.reshape(depth, n_pool, page * SB_HEADS, SB_HEAD_DIM)
    w_in_t = jnp.swapaxes(w_in, 1, 2)
    wts = (w_ffn1_in, w_ffn1_out, w_in_t, w_gate, w_branch_attn, w_branch_ssm, w_out, w_ffn2_in, w_ffn2_out)

    outs = {k: [] for k in ("kp", "vp", "hp", "cp", "ks", "vs", "hs", "cs")}
    for l in range(depth):
        mod = _mm_bias(c_all, w_mod, b_mod[l][None, :], l, col0=0, n=N_MOD * d, tm=c_rows, tn=512,
                       pre_silu=True, name=f"mod_l{l}").reshape(c_rows, N_MOD, d)
        small = (norm_g[l], conv_w[l], conv_b[l], dt_bias[l], a_log[l], d_skip[l], ssm_norm_g[l], b_gate[l])

        def attn_p(qkv):
            return _attn_prompt(qkv.reshape(bp, tp, 3 * D_ATTN), sb_bias[l]).reshape(bp * tp, D_ATTN)

        xp, kv_p, zx_p, h_p = _layer(xp, mod[:bp], attn_p, h0_prompt, conv0_prompt, wts, small, l,
                                     rows_per_seq=tp, tm=min(tp, 1024), tn_ffn=256, split_qkv=True, tag=f"p{l}")
        outs["kp"].append(kv_p[0].reshape(bp, tp, SB_HEADS, SB_HEAD_DIM))
        outs["vp"].append(kv_p[1].reshape(bp, tp, SB_HEADS, SB_HEAD_DIM))
        outs["hp"].append(h_p.reshape(bp, n_heads, SSM_HEAD_DIM, SSM_STATE))
        outs["cp"].append(zx_p[:, tp - (CONV_WIDTH - 1):, d_ssm:])

        def attn_s(qkv):
            return _attn_sample(qkv, ck, cv, page_table, sb_bias[l], l, tq=ts)

        xs, kv_s, zx_s, h_s = _layer(xs, mod[bp:bp + bs], attn_s, state_ssm[l].reshape(bs, d_ssm, SSM_STATE),
                                     state_conv[l], wts, small, l, rows_per_seq=ts, tm=bs * ts, tn_ffn=256,
                                     split_qkv=False, tag=f"s{l}")
        outs["ks"].append(kv_s[0].reshape(bs, ts, SB_HEADS, SB_HEAD_DIM))
        outs["vs"].append(kv_s[1].reshape(bs, ts, SB_HEADS, SB_HEAD_DIM))
        outs["hs"].append(h_s.reshape(bs, n_heads, SSM_HEAD_DIM, SSM_STATE))
        conv_in = jnp.concatenate([state_conv[l], zx_s[:, :ts, d_ssm:]], axis=1)
        outs["cs"].append(conv_in[:, -(CONV_WIDTH - 1):])

    y_prompt = _rmsnorm(xp, final_norm_g[None, :], tm=256, name="final_norm_p").reshape(bp, tp, d)
    y_sample = _rmsnorm(xs, final_norm_g[None, :], tm=bs * ts, name="final_norm_s").reshape(bs, ts, d)
    st = lambda k: jnp.stack(outs[k])
    return (y_prompt, y_sample, st("kp"), st("vp"), st("hp"), st("cp"), st("ks"), st("vs"), st("hs"), st("cs"))
```

```python
import functools

import jax
import jax.numpy as jnp
from jax import lax
from jax.experimental import pallas as pl
from jax.experimental.pallas import tpu as pltpu

F32 = jnp.float32
BF16 = jnp.bfloat16

SB_HEADS = 16
HEADS_SHIFT = 4
TQ_SHIFT = 3
SB_HEAD_DIM = 128
D_ATTN = SB_HEADS * SB_HEAD_DIM
SSM_HEAD_DIM = 64
HEAD_DIM_SHIFT = 6
SSM_GROUPS = 8
SSM_STATE = 128
CONV_WIDTH = 4
N_MOD = 9
EPS = 1e-6
CHUNK = 128
LANES = 128
SUBLANES = 8
VMEM_LIMIT = 56 * 1024 * 1024

ATTN_BLOCK = 256
ATTN_HEADS_PER_STEP = 8
PAGES_PER_STEP = 8
CONV_ROWS = 512
SSD_GROUPS_PER_STEP = 8


def _params(*sem):
    return pltpu.CompilerParams(dimension_semantics=sem, vmem_limit_bytes=VMEM_LIMIT)


def _wspec(block, imap, layer):
    return pl.BlockSpec((None,) + block, lambda *a: (layer,) + imap(*a))


def _sigmoid(x):
    return 1.0 / (1.0 + jnp.exp(-x))


def _silu(x):
    return x * _sigmoid(x)


def _softplus(x):
    return jnp.maximum(x, 0.0) + jnp.log1p(jnp.exp(-jnp.abs(x)))


def _bdot(a, b):
    return jnp.dot(a.astype(BF16), b.astype(BF16), preferred_element_type=F32)


def _bdot_nt(a, b):
    return lax.dot_general(a.astype(BF16), b.astype(BF16), (((1,), (1,)), ((), ())),
                           preferred_element_type=F32)


def _split3(x):
    hi = x.astype(BF16)
    r1 = x - hi.astype(F32)
    mid = r1.astype(BF16)
    lo = (r1 - mid.astype(F32)).astype(BF16)
    return hi, mid, lo


def _exact_dot(x, m01):
    hi, mid, lo = _split3(x)
    d = functools.partial(jnp.dot, preferred_element_type=F32)
    return d(hi, m01) + d(mid, m01) + d(lo, m01)


def _mm_bias_kernel(x_ref, w_ref, b_ref, o_ref, *, pre_silu, w_is_nk):
    x = x_ref[...]
    if pre_silu:
        x = _silu(x.astype(F32))
    o_ref[...] = (_bdot_nt if w_is_nk else _bdot)(x, w_ref[...]) + b_ref[...]


def _mm_bias(x, w, b, layer, *, col0, n, tm, tn, pre_silu=False, w_is_nk=False, name):
    m, k = x.shape
    c0 = col0 // tn
    w_spec = (_wspec((tn, k), lambda i, j: (c0 + j, 0), layer) if w_is_nk else
              _wspec((k, tn), lambda i, j: (0, c0 + j), layer))
    return pl.pallas_call(
        functools.partial(_mm_bias_kernel, pre_silu=pre_silu, w_is_nk=w_is_nk),
        grid=(m // tm, n // tn),
        in_specs=[pl.BlockSpec((tm, k), lambda i, j: (i, 0)),
                  w_spec,
                  pl.BlockSpec((1, tn), lambda i, j: (0, c0 + j))],
        out_specs=pl.BlockSpec((tm, tn), lambda i, j: (i, j)),
        out_shape=jax.ShapeDtypeStruct((m, n), F32),
        compiler_params=_params("parallel", "arbitrary"),
        name=name,
    )(x, w, b)


def _mm_qkv_kernel(x_ref, w_ref, *o_refs, nq, scale):
    j = pl.program_id(1)
    acc = _bdot_nt(x_ref[...], w_ref[...])
    o_refs[0][...] = (acc * jnp.where(j < nq, scale, 1.0)).astype(o_refs[0].dtype)
    if len(o_refs) > 1:
        @pl.when((j >= nq) & (j < 2 * nq))
        def _():
            o_refs[1][...] = acc

        @pl.when(j >= 2 * nq)
        def _():
            o_refs[2][...] = acc


def _mm_qkv(x, w, layer, *, tm, tn, scale, split, name):
    m, k = x.shape
    nq = D_ATTN // tn
    out_specs = [pl.BlockSpec((tm, tn), lambda i, j: (i, j))]
    out_shape = [jax.ShapeDtypeStruct((m, 3 * D_ATTN), BF16 if split else F32)]
    if split:
        for sec in (1, 2):
            out_specs.append(pl.BlockSpec(
                (tm, tn), functools.partial(lambda sec, i, j: (i, jnp.clip(j - sec * nq, 0, nq - 1)), sec)))
            out_shape.append(jax.ShapeDtypeStruct((m, D_ATTN), F32))
    outs = pl.pallas_call(
        functools.partial(_mm_qkv_kernel, nq=nq, scale=scale),
        grid=(m // tm, 3 * nq),
        in_specs=[pl.BlockSpec((tm, k), lambda i, j: (i, 0)),
                  _wspec((tn, k), lambda i, j: (j, 0), layer)],
        out_specs=out_specs,
        out_shape=out_shape,
        compiler_params=_params("parallel", "arbitrary"),
        name=name,
    )(x, w)
    return outs if split else outs[0]


def _mm_sigmoid_kernel(x_ref, w_ref, b_ref, o_ref):
    acc = _bdot(x_ref[...], w_ref[...])
    o_ref[...] = _sigmoid(acc + b_ref[...]).astype(o_ref.dtype)


def _mm_sigmoid(x, w, b, layer, *, tm, tn, name):
    m, k = x.shape
    n = w.shape[2]
    return pl.pallas_call(
        _mm_sigmoid_kernel,
        grid=(m // tm, n // tn),
        in_specs=[pl.BlockSpec((tm, k), lambda i, j: (i, 0)),
                  _wspec((k, tn), lambda i, j: (0, j), layer),
                  pl.BlockSpec((1, tn), lambda i, j: (0, j))],
        out_specs=pl.BlockSpec((tm, tn), lambda i, j: (i, j)),
        out_shape=jax.ShapeDtypeStruct((m, n), BF16),
        compiler_params=_params("parallel", "arbitrary"),
        name=name,
    )(x, w, b)


def _mm_swiglu_kernel(x_ref, wg_ref, wu_ref, o_ref):
    x = x_ref[...]
    gate = _bdot(x, wg_ref[...])
    up = _bdot(x, wu_ref[...])
    o_ref[...] = (_silu(gate) * up).astype(o_ref.dtype)


def _mm_swiglu(x, w, layer, *, tm, tn, name):
    m, k = x.shape
    f = w.shape[2] // 2
    nb = f // tn
    return pl.pallas_call(
        _mm_swiglu_kernel,
        grid=(m // tm, nb),
        in_specs=[pl.BlockSpec((tm, k), lambda i, j: (i, 0)),
                  _wspec((k, tn), lambda i, j: (0, j), layer),
                  _wspec((k, tn), lambda i, j: (0, nb + j), layer)],
        out_specs=pl.BlockSpec((tm, tn), lambda i, j: (i, j)),
        out_shape=jax.ShapeDtypeStruct((m, f), BF16),
        compiler_params=_params("parallel", "arbitrary"),
        name=name,
    )(x, w, w)


def _mm_resid_kernel(x_ref, w_ref, r_ref, g_ref, o_ref, *, scale):
    acc = _bdot(x_ref[...], w_ref[...])
    o_ref[...] = r_ref[...] + (scale * g_ref[0]) * acc


def _mm_resid(x, w, res, gate, layer, *, scale, tm, tn, rows_per_gate, name, single_buffer_x=False):
    m, k = x.shape
    n = w.shape[2]
    r = gate.shape[1]
    if r == 1:
        assert rows_per_gate % tm == 0
        gmap = lambda i, j: ((i * tm) // rows_per_gate, 0, j)
    else:
        assert r == tm
        gmap = lambda i, j: (i, 0, j)
    x_mode = dict(pipeline_mode=pl.Buffered(1)) if single_buffer_x else {}
    return pl.pallas_call(
        functools.partial(_mm_resid_kernel, scale=scale),
        grid=(m // tm, n // tn),
        in_specs=[pl.BlockSpec((tm, k), lambda i, j: (i, 0), **x_mode),
                  _wspec((k, tn), lambda i, j: (0, j), layer),
                  pl.BlockSpec((tm, tn), lambda i, j: (i, j)),
                  pl.BlockSpec((1, r, tn), gmap)],
        out_specs=pl.BlockSpec((tm, tn), lambda i, j: (i, j)),
        out_shape=jax.ShapeDtypeStruct((m, n), F32),
        compiler_params=_params("parallel", "arbitrary"),
        name=name,
    )(x, w, res, gate)


def _mm_merge_kernel(oa_ref, wa_ref, os_ref, ws_ref, ga_ref, gs_ref, o_ref):
    a = _bdot(oa_ref[...], wa_ref[...])
    s = _bdot(os_ref[...], ws_ref[...])
    o_ref[...] = (ga_ref[...].astype(F32) * a + gs_ref[...].astype(F32) * s).astype(o_ref.dtype)


def _mm_merge(oa, wa, os_, ws, gates, layer, *, tm, tn, name):
    m = oa.shape[0]
    n = wa.shape[2]
    nb = n // tn
    return pl.pallas_call(
        _mm_merge_kernel,
        grid=(m // tm, nb),
        in_specs=[pl.BlockSpec((tm, oa.shape[1]), lambda i, j: (i, 0)),
                  _wspec((wa.shape[1], tn), lambda i, j: (0, j), layer),
                  pl.BlockSpec((tm, os_.shape[1]), lambda i, j: (i, 0)),
                  _wspec((ws.shape[1], tn), lambda i, j: (0, j), layer),
                  pl.BlockSpec((tm, tn), lambda i, j: (i, j)),
                  pl.BlockSpec((tm, tn), lambda i, j: (i, nb + j))],
        out_specs=pl.BlockSpec((tm, tn), lambda i, j: (i, j)),
        out_shape=jax.ShapeDtypeStruct((m, n), BF16),
        compiler_params=_params("parallel", "arbitrary"),
        name=name,
    )(oa, wa, os_, ws, gates, gates)


def _modnorm_kernel(x_ref, g_ref, sh_ref, sc_ref, o_ref):
    x = x_ref[...]
    y = x * lax.rsqrt(jnp.mean(x * x, axis=-1, keepdims=True) + EPS)
    y = y * g_ref[...]
    o_ref[...] = (y * (1.0 + sc_ref[0]) + sh_ref[0]).astype(o_ref.dtype)


def _modnorm(x, g, shift, scale, *, rows_per_seq, tm, name):
    m, d = x.shape
    r = shift.shape[1]
    if r == 1:
        assert rows_per_seq % tm == 0
        smap = lambda i: ((i * tm) // rows_per_seq, 0, 0)
    else:
        assert r == tm
        smap = lambda i: (i, 0, 0)
    return pl.pallas_call(
        _modnorm_kernel,
        grid=(m // tm,),
        in_specs=[pl.BlockSpec((tm, d), lambda i: (i, 0)),
                  pl.BlockSpec((1, d), lambda i: (0, 0)),
                  pl.BlockSpec((1, r, d), smap),
                  pl.BlockSpec((1, r, d), smap)],
        out_specs=pl.BlockSpec((tm, d), lambda i: (i, 0)),
        out_shape=jax.ShapeDtypeStruct((m, d), BF16),
        compiler_params=_params("parallel"),
        name=name,
    )(x, g, shift, scale)


def _rmsnorm_kernel(x_ref, g_ref, o_ref):
    x = x_ref[...]
    y = x * lax.rsqrt(jnp.mean(x * x, axis=-1, keepdims=True) + EPS)
    o_ref[...] = y * g_ref[...]


def _rmsnorm(x, g, *, tm, name):
    m, d = x.shape
    return pl.pallas_call(
        _rmsnorm_kernel,
        grid=(m // tm,),
        in_specs=[pl.BlockSpec((tm, d), lambda i: (i, 0)),
                  pl.BlockSpec((1, d), lambda i: (0, 0))],
        out_specs=pl.BlockSpec((tm, d), lambda i: (i, 0)),
        out_shape=jax.ShapeDtypeStruct((m, d), F32),
        compiler_params=_params("parallel"),
        name=name,
    )(x, g)


def _cumsum_rhs():
    j = lax.broadcasted_iota(jnp.int32, (2 * LANES, 2 * LANES), 0) & (LANES - 1)
    s = lax.broadcasted_iota(jnp.int32, (2 * LANES, 2 * LANES), 1)
    return jnp.where((j > s) | (s >= LANES), 1.0, 0.0).astype(BF16)


def _sb_weights(z_tiles, vis_tiles, carry, rhs):
    ls_t, part_t = [], []
    for z, vis in zip(z_tiles, vis_tiles):
        t = jnp.log(1.0 + jnp.exp(-jnp.abs(z)))
        ls_t.append(jnp.minimum(z, 0.0) - t)
        lk = -jnp.maximum(z, 0.0) - t
        if vis is not None:
            lk = jnp.where(vis, lk, 0.0)
        hi = lk.astype(BF16)
        lo = (lk - hi.astype(F32)).astype(BF16)
        part_t.append(jnp.dot(jnp.concatenate([hi, lo], axis=1), rhs, preferred_element_type=F32))
    w_t = [None] * len(z_tiles)
    for i in range(len(z_tiles) - 1, -1, -1):
        later = carry + part_t[i][:, :LANES]
        w = jnp.exp(ls_t[i] + later)
        if vis_tiles[i] is not None:
            w = jnp.where(vis_tiles[i], w, 0.0)
        w_t[i] = w
        carry = carry + part_t[i][:, LANES:]
    return w_t, carry


def _attn_prompt_kernel(bias_ref, q_ref, k_ref, v_ref, o_ref, acc_ref, carry_ref):
    hg = pl.program_id(1)
    qi = pl.program_id(2)
    nh = ATTN_HEADS_PER_STEP
    nt = ATTN_BLOCK // LANES
    rhs = _cumsum_rhs()
    row = lax.broadcasted_iota(jnp.int32, (ATTN_BLOCK, LANES), 0)
    col = lax.broadcasted_iota(jnp.int32, (ATTN_BLOCK, LANES), 1)
    diag_vis = [col + t * LANES < row for t in range(nt)]

    def hcols(h):
        return slice(h * SB_HEAD_DIM, (h + 1) * SB_HEAD_DIM)

    def block(kb, vis_tiles, first):
        start = pl.multiple_of(kb * ATTN_BLOCK, ATTN_BLOCK)
        for h in range(nh):
            bias = bias_ref[hg * nh + h]
            q = q_ref[0, :, hcols(h)]
            kblk = k_ref[0, pl.ds(start, ATTN_BLOCK), hcols(h)]
            vblk = v_ref[0, pl.ds(start, ATTN_BLOCK), hcols(h)]
            z = _bdot_nt(q, kblk) + bias
            carry = jnp.zeros((ATTN_BLOCK, LANES), F32) if first else carry_ref[h]
            w_t, carry = _sb_weights([z[:, t * LANES:(t + 1) * LANES] for t in range(nt)], vis_tiles, carry, rhs)
            carry_ref[h] = carry
            pv = jnp.dot(jnp.concatenate([w.astype(BF16) for w in w_t], axis=1), vblk, preferred_element_type=F32)
            acc_ref[h] = pv if first else acc_ref[h] + pv

    block(qi, diag_vis, True)

    def body(it, _):
        block(qi - 1 - it, [None] * nt, False)
        return 0

    lax.fori_loop(0, qi, body, 0)
    for h in range(nh):
        o_ref[0, :, hcols(h)] = acc_ref[h].astype(o_ref.dtype)


def _attn_prompt(qkv, sb_bias):
    b, t, _ = qkv.shape
    nq = t // ATTN_BLOCK
    nh = ATTN_HEADS_PER_STEP
    w = nh * SB_HEAD_DIM
    ng = SB_HEADS // nh
    return pl.pallas_call(
        _attn_prompt_kernel,
        grid=(b, ng, nq),
        in_specs=[pl.BlockSpec(memory_space=pltpu.SMEM),
                  pl.BlockSpec((1, ATTN_BLOCK, w), lambda bi, h, qi: (bi, qi, h)),
                  pl.BlockSpec((1, t, w), lambda bi, h, qi: (bi, 0, ng + h)),
                  pl.BlockSpec((1, t, w), lambda bi, h, qi: (bi, 0, 2 * ng + h))],
        out_specs=pl.BlockSpec((1, ATTN_BLOCK, w), lambda bi, h, qi: (bi, qi, h)),
        out_shape=jax.ShapeDtypeStruct((b, t, D_ATTN), BF16),
        scratch_shapes=[pltpu.VMEM((nh, ATTN_BLOCK, SB_HEAD_DIM), F32),
                        pltpu.VMEM((nh, ATTN_BLOCK, LANES), F32)],
        compiler_params=_params("parallel", "parallel", "arbitrary"),
        name="attn_prompt",
    )(sb_bias, qkv, qkv, qkv)


def _attn_sample_kernel(pt_ref, bias_ref, q_ref, kn_ref, vn_ref, *refs, tq):
    npg = PAGES_PER_STEP
    kc_refs, vc_refs = refs[:npg], refs[npg:2 * npg]
    o_ref, acc_ref, carry_ref, spread_ref, own_ref = refs[2 * npg:]
    s = pl.program_id(1)
    rows = SB_HEADS * tq
    page_rows = CHUNK * SB_HEADS
    rhs = _cumsum_rhs()

    def hcols(h):
        return slice(h * SB_HEAD_DIM, (h + 1) * SB_HEAD_DIM)

    def process(k_tiles, pv, vis):
        z_t = [jnp.concatenate([_bdot_nt(q_ref[:, hcols(h)], k_of(h)) + bias_ref[h] for h in range(SB_HEADS)], axis=0)
               for k_of in k_tiles]
        w_t, carry = _sb_weights(z_t, [vis] * len(z_t), carry_ref[...], rhs)
        carry_ref[...] = carry
        pv(w_t)

    @pl.when(s == 0)
    def _():
        carry_ref[...] = jnp.zeros_like(carry_ref)
        acc_ref[...] = jnp.zeros_like(acc_ref)
        tok = lax.broadcasted_iota(jnp.int32, (rows, page_rows), 0)
        r_i = lax.broadcasted_iota(jnp.int32, (rows, page_rows), 1)
        spread_ref[...] = jnp.where(lax.shift_right_logical(r_i, HEADS_SHIFT) == tok, 1.0, 0.0).astype(BF16)
        own_ref[...] = jnp.where((r_i & (SB_HEADS - 1)) == lax.shift_right_logical(tok, TQ_SHIFT), 1.0, 0.0).astype(BF16)

        pad = jnp.zeros((CHUNK - tq, SB_HEAD_DIM), F32)
        r_idx = lax.broadcasted_iota(jnp.int32, (rows, CHUNK), 0) & (tq - 1)
        s_idx = lax.broadcasted_iota(jnp.int32, (rows, CHUNK), 1)

        def pv_new(w_t):
            for h in range(SB_HEADS):
                v_h = jnp.concatenate([vn_ref[:, hcols(h)], pad], axis=0)
                acc_ref[h * tq:(h + 1) * tq, :] += _bdot(w_t[0][h * tq:(h + 1) * tq, :], v_h)

        process([lambda h: jnp.concatenate([kn_ref[:, hcols(h)], pad], axis=0)], pv_new, s_idx < r_idx)

    def head_rows(r, h):
        return r[pl.ds(h, CHUNK, stride=SB_HEADS), :]

    def pv_pages(w_t):
        upd = jnp.zeros((rows, SB_HEAD_DIM), F32)
        for w, v_ref in zip(w_t, reversed(vc_refs)):
            wide = jnp.dot(w.astype(BF16), spread_ref[...], preferred_element_type=F32)
            upd = upd + jnp.dot(wide.astype(BF16) * own_ref[...], v_ref[...].astype(BF16),
                                preferred_element_type=F32)
        acc_ref[...] += upd

    process([functools.partial(head_rows, r) for r in reversed(kc_refs)], pv_pages, None)

    @pl.when(s == pl.num_programs(1) - 1)
    def _():
        for h in range(SB_HEADS):
            o_ref[:, hcols(h)] = acc_ref[h * tq:(h + 1) * tq, :].astype(o_ref.dtype)


def _attn_sample(qkv, cache_k, cache_v, page_table, sb_bias, layer, *, tq):
    bs, n_pages = page_table.shape
    page_rows = cache_k.shape[2]
    npg = PAGES_PER_STEP
    assert page_rows == CHUNK * SB_HEADS and tq == SUBLANES == 1 << TQ_SHIFT and n_pages % npg == 0
    new_specs = [pl.BlockSpec((tq, D_ATTN), functools.partial(lambda sec, b, s, pt: (b, sec), sec))
                 for sec in range(3)]

    def page_spec(i):
        return pl.BlockSpec((None, None, page_rows, SB_HEAD_DIM),
                            lambda b, s, pt: (layer, pt[b, n_pages - 1 - (s * npg + i)], 0, 0))

    return pl.pallas_call(
        functools.partial(_attn_sample_kernel, tq=tq),
        grid_spec=pltpu.PrefetchScalarGridSpec(
            num_scalar_prefetch=1,
            grid=(bs, n_pages // npg),
            in_specs=[pl.BlockSpec(memory_space=pltpu.SMEM)] + new_specs + [page_spec(i) for i in range(npg)] * 2,
            out_specs=pl.BlockSpec((tq, D_ATTN), lambda b, s, pt: (b, 0)),
            scratch_shapes=[pltpu.VMEM((SB_HEADS * tq, SB_HEAD_DIM), F32),
                            pltpu.VMEM((SB_HEADS * tq, LANES), F32),
                            pltpu.VMEM((SB_HEADS * tq, page_rows), BF16),
                            pltpu.VMEM((SB_HEADS * tq, page_rows), BF16)],
        ),
        out_shape=jax.ShapeDtypeStruct((bs * tq, D_ATTN), F32),
        compiler_params=_params("parallel", "arbitrary"),
        name="attn_sample",
    )(page_table, sb_bias, qkv, qkv, qkv, *([cache_k] * npg), *([cache_v] * npg))


def _conv_kernel(cur_ref, prev_ref, buf_ref, w_ref, b_ref, o_ref, xp_ref):
    c = pl.program_id(1)
    rows = cur_ref.shape[1]
    prev = jnp.where(c == 0, buf_ref[0], prev_ref[0])
    xp_ref[0:SUBLANES, :] = prev
    xp_ref[SUBLANES:, :] = cur_ref[0]
    out = b_ref[...] + xp_ref[pl.ds(SUBLANES - CONV_WIDTH + 1, rows), :] * w_ref[0:1, :]
    for i in range(1, CONV_WIDTH):
        out = out + xp_ref[pl.ds(SUBLANES - CONV_WIDTH + 1 + i, rows), :] * w_ref[i:i + 1, :]
    o_ref[0] = _silu(out)


def _conv(zx, buf8, conv_w, conv_b, *, col0, tc, name):
    b, t, _ = zx.shape
    ch = conv_w.shape[1]
    c0 = col0 // tc
    rows = min(t, CONV_ROWS)
    per = rows // SUBLANES
    return pl.pallas_call(
        _conv_kernel,
        grid=(b, t // rows, ch // tc),
        in_specs=[pl.BlockSpec((1, rows, tc), lambda bi, c, j: (bi, c, c0 + j)),
                  pl.BlockSpec((1, SUBLANES, tc), lambda bi, c, j: (bi, jnp.maximum(c * per - 1, 0), c0 + j)),
                  pl.BlockSpec((1, SUBLANES, tc), lambda bi, c, j: (bi, 0, j)),
                  pl.BlockSpec((CONV_WIDTH, tc), lambda bi, c, j: (0, j)),
                  pl.BlockSpec((1, tc), lambda bi, c, j: (0, j))],
        out_specs=pl.BlockSpec((1, rows, tc), lambda bi, c, j: (bi, c, j)),
        out_shape=jax.ShapeDtypeStruct((b, t, ch), F32),
        scratch_shapes=[pltpu.VMEM((rows + SUBLANES, tc), F32)],
        compiler_params=_params("parallel", "parallel", "parallel"),
        name=name,
    )(zx, zx, buf8, conv_w, conv_b)


def _ssd_kernel(xs_ref, bm_ref, cm_ref, z_ref, dt_ref, dtt_ref, h0_ref, dtb_row_ref, dtb_col_ref, alog_row_ref,
                alog_col_ref, dskip_ref, ng_ref, y_ref, hout_ref, ht_ref, *, t_valid):
    c = pl.program_id(2)
    gps = bm_ref.shape[2] // SSM_STATE
    gw = xs_ref.shape[2] // gps
    hpg = gw // SSM_HEAD_DIM
    f32dot = functools.partial(jnp.dot, preferred_element_type=F32)

    @pl.when(c == 0)
    def _():
        ht_ref[...] = h0_ref[0].T

    row = lax.broadcasted_iota(jnp.int32, (CHUNK, CHUNK), 0)
    col = lax.broadcasted_iota(jnp.int32, (CHUNK, CHUNK), 1)
    causal = row >= col
    tri_incl = jnp.where(causal, 1.0, 0.0).astype(BF16)
    tri_incl_t = jnp.where(col >= row, 1.0, 0.0).astype(BF16)
    e_row = lax.broadcasted_iota(jnp.int32, (LANES, gw), 0)
    e_col = lax.broadcasted_iota(jnp.int32, (LANES, gw), 1)
    t_col = c * CHUNK + lax.broadcasted_iota(jnp.int32, (CHUNK, gw), 0)
    t_row = c * CHUNK + lax.broadcasted_iota(jnp.int32, (hpg, CHUNK), 1)
    lane = lax.broadcasted_iota(jnp.int32, (CHUNK, LANES), 1)
    first_head = lane < SSM_HEAD_DIM
    dt_heads = jnp.where(lane < SSM_GROUPS * hpg, dt_ref[0], 0.0)

    for u in range(gps):
        g = pl.program_id(1) * gps + u
        ch = slice(u * gw, (u + 1) * gw)
        st = slice(u * SSM_STATE, (u + 1) * SSM_STATE)

        expand = jnp.where(e_row == g * hpg + lax.shift_right_logical(e_col, HEAD_DIM_SHIFT), 1.0, 0.0).astype(BF16)

        dt_raw = _exact_dot(dt_heads, expand)
        dt = jnp.where(t_col < t_valid, _softplus(dt_raw + dtb_row_ref[:, ch]), 0.0)
        a_row = -jnp.exp(alog_row_ref[:, ch])
        hi, mid, lo = _split3(dt * a_row)
        a_cum = f32dot(tri_incl, hi) + f32dot(tri_incl, mid) + f32dot(tri_incl, lo)
        a_end = a_cum[CHUNK - 1:CHUNK, :]

        h_lo = pl.multiple_of(g * hpg, hpg)
        dt_r = jnp.where(t_row < t_valid,
                         _softplus(dtt_ref[0, pl.ds(h_lo, hpg), :] + dtb_col_ref[pl.ds(h_lo, hpg), :]), 0.0)
        hi, mid, lo = _split3(dt_r * (-jnp.exp(alog_col_ref[pl.ds(h_lo, hpg), :])))
        a_cum_r = f32dot(hi, tri_incl_t) + f32dot(mid, tri_incl_t) + f32dot(lo, tri_incl_t)

        xs = xs_ref[0, :, ch]
        bm = bm_ref[0, :, st]
        cm = cm_ref[0, :, st].astype(BF16)
        xdt = xs * dt
        xdt_b = xdt.astype(BF16)
        cb = _bdot_nt(cm, bm)

        ht = ht_ref[:, ch]
        y = f32dot(cm, ht.astype(BF16)) * jnp.exp(a_cum)
        y_diag = []
        for p in range(hpg // 2):
            cols = slice(p * LANES, (p + 1) * LANES)
            scores = []
            for j in (2 * p, 2 * p + 1):
                a_l = jnp.broadcast_to(a_cum[:, j * SSM_HEAD_DIM:j * SSM_HEAD_DIM + 1], (CHUNK, CHUNK))
                seg = a_l - a_cum_r[j:j + 1, :]
                scores.append((cb * jnp.exp(jnp.where(causal, seg, -jnp.inf))).astype(BF16))
            xp = xdt_b[:, cols]
            rhs = jnp.concatenate([jnp.where(first_head, xp, jnp.zeros_like(xp)),
                                   jnp.where(first_head, jnp.zeros_like(xp), xp)], axis=0)
            y_diag.append(f32dot(jnp.concatenate(scores, axis=1), rhs))
        y = y + jnp.concatenate(y_diag, axis=1)

        xd = (xdt * jnp.exp(a_end - a_cum)).astype(BF16)
        ht_ref[:, ch] = ht * jnp.exp(a_end) + f32dot(bm.T.astype(BF16), xd)

        y = y + xs * dskip_ref[:, ch]
        y = y * _silu(z_ref[0, :, ch])
        y = y * lax.rsqrt(jnp.mean(y * y, axis=-1, keepdims=True) + EPS)
        y_ref[0, :, ch] = (y * ng_ref[:, ch]).astype(y_ref.dtype)

    @pl.when(c == pl.num_programs(2) - 1)
    def _():
        hout_ref[0] = ht_ref[...].T


def _ssd(xc, zx, dt, dtt, h0, dtb_row, dtb_col, alog_row, alog_col, dskip, ng, *, t_valid, name):
    b, t, cc = xc.shape
    d_ssm = cc - 2 * SSM_GROUPS * SSM_STATE
    gps = SSD_GROUPS_PER_STEP
    gw = gps * d_ssm // SSM_GROUPS
    sw = gps * SSM_STATE
    nb0 = d_ssm // sw
    assert d_ssm % sw == 0 and SSM_GROUPS % gps == 0
    gmap = lambda bi, g, c: (0, g)
    return pl.pallas_call(
        functools.partial(_ssd_kernel, t_valid=t_valid),
        grid=(b, SSM_GROUPS // gps, t // CHUNK),
        in_specs=[pl.BlockSpec((1, CHUNK, gw), lambda bi, g, c: (bi, c, g)),
                  pl.BlockSpec((1, CHUNK, sw), lambda bi, g, c: (bi, c, nb0 + g)),
                  pl.BlockSpec((1, CHUNK, sw), lambda bi, g, c: (bi, c, nb0 + SSM_GROUPS // gps + g)),
                  pl.BlockSpec((1, CHUNK, gw), lambda bi, g, c: (bi, c, g)),
                  pl.BlockSpec((1, CHUNK, LANES), lambda bi, g, c: (bi, c, 0)),
                  pl.BlockSpec((1, LANES, CHUNK), lambda bi, g, c: (bi, 0, c)),
                  pl.BlockSpec((1, gw, SSM_STATE), lambda bi, g, c: (bi, g, 0)),
                  pl.BlockSpec((1, gw), gmap),
                  pl.BlockSpec((LANES, 1), lambda bi, g, c: (0, 0)),
                  pl.BlockSpec((1, gw), gmap),
                  pl.BlockSpec((LANES, 1), lambda bi, g, c: (0, 0)),
                  pl.BlockSpec((1, gw), gmap),
                  pl.BlockSpec((1, gw), gmap)],
        out_specs=[pl.BlockSpec((1, CHUNK, gw), lambda bi, g, c: (bi, c, g)),
                   pl.BlockSpec((1, gw, SSM_STATE), lambda bi, g, c: (bi, g, 0))],
        out_shape=[jax.ShapeDtypeStruct((b, t, d_ssm), BF16),
                   jax.ShapeDtypeStruct((b, d_ssm, SSM_STATE), F32)],
        scratch_shapes=[pltpu.VMEM((SSM_STATE, gw), F32)],
        compiler_params=_params("parallel", "parallel", "arbitrary"),
        name=name,
    )(xc, xc, xc, zx, dt, dtt, h0, dtb_row, dtb_col, alog_row, alog_col, dskip, ng)


def _pad_heads(v):
    return jnp.pad(v, (0, LANES - v.shape[0]))


def _layer(x, mod, attn_fn, h0, conv_buf, wts, small, layer, *, rows_per_seq, tm, tn_ffn, split_qkv, tag):
    m, d = x.shape
    nseq = m // rows_per_seq
    (w_ffn1_in, w_ffn1_out, w_in_t, w_gate, w_branch_attn, w_branch_ssm, w_out, w_ffn2_in, w_ffn2_out) = wts
    (norm_g, conv_w, conv_b, dt_bias, a_log, d_skip, ssm_norm_g, b_gate) = small
    per_seq = rows_per_seq % tm == 0
    tnorm = 256 if per_seq else m
    big_ffn_out = tm * w_ffn1_out.shape[1] * 2 > 16 * 1024 * 1024
    d_ssm = w_branch_ssm.shape[1]
    conv_ch = conv_w.shape[1]
    rep = lambda v: jnp.repeat(v, SSM_HEAD_DIM)[None, :]
    nm = lambda s: f"{s}_{tag}"

    def mvec(idx):
        v = mod[:, idx:idx + 1, :]
        return v if per_seq else jnp.repeat(v[:, 0, :], rows_per_seq, axis=0)[None]

    def modnorm(x, k):
        return _modnorm(x, norm_g[k:k + 1], mvec(3 * k), mvec(3 * k + 1), rows_per_seq=rows_per_seq, tm=tnorm,
                        name=nm(f"modnorm{k}"))

    def ffn(x, k, w_a, w_b):
        hdn = _mm_swiglu(modnorm(x, k), w_a, layer, tm=tm, tn=tn_ffn, name=nm(f"ffn{k}_in"))
        return _mm_resid(hdn, w_b, x, mvec(3 * k + 2), layer, scale=0.5, tm=tm, tn=tn_ffn,
                         rows_per_gate=rows_per_seq, name=nm(f"ffn{k}_out"), single_buffer_x=big_ffn_out)

    x = ffn(x, 0, w_ffn1_in, w_ffn1_out)

    n = modnorm(x, 1)
    zero_bias = jnp.zeros((1, w_in_t.shape[1]), F32)
    qkv = _mm_qkv(n, w_in_t, layer, tm=tm, tn=512, scale=SB_HEAD_DIM ** -0.5, split=split_qkv, name=nm("proj_qkv"))
    zx = _mm_bias(n, w_in_t, zero_bias, layer, col0=3 * D_ATTN, n=d_ssm + conv_ch, tm=tm, tn=512, w_is_nk=True,
                  name=nm("proj_zx"))
    dt_raw = _mm_bias(n, w_in_t, zero_bias, layer, col0=3 * D_ATTN + d_ssm + conv_ch, n=LANES, tm=tm, tn=LANES,
                      w_is_nk=True, name=nm("proj_dt"))
    gates = _mm_sigmoid(n, w_gate, b_gate[None, :], layer, tm=tm, tn=512, name=nm("gates"))

    o_attn = attn_fn(qkv[0] if split_qkv else qkv)
    kv = qkv[1:] if split_qkv else (qkv[:, D_ATTN:2 * D_ATTN], qkv[:, 2 * D_ATTN:])

    t_pad = -(-rows_per_seq // CHUNK) * CHUNK
    zx3 = zx.reshape(nseq, rows_per_seq, d_ssm + conv_ch)
    dt3 = dt_raw.reshape(nseq, rows_per_seq, LANES)
    if t_pad != rows_per_seq:
        padt = ((0, 0), (0, t_pad - rows_per_seq), (0, 0))
        zx3, dt3 = jnp.pad(zx3, padt), jnp.pad(dt3, padt)
    buf8 = jnp.pad(conv_buf, ((0, 0), (SUBLANES - conv_buf.shape[1], 0), (0, 0)))
    xc = _conv(zx3, buf8, conv_w, conv_b[None, :], col0=d_ssm, tc=1024, name=nm("conv"))
    y_ssm, h_new = _ssd(xc, zx3, dt3, jnp.swapaxes(dt3, 1, 2), h0,
                        rep(dt_bias), _pad_heads(dt_bias)[:, None], rep(a_log), _pad_heads(a_log)[:, None],
                        rep(d_skip), ssm_norm_g[None, :], t_valid=rows_per_seq, name=nm("ssd"))
    o_ssm = y_ssm[:, :rows_per_seq].reshape(m, d_ssm)

    merged = _mm_merge(o_attn, w_branch_attn, o_ssm, w_branch_ssm, gates, layer, tm=tm, tn=256, name=nm("merge"))
    x = _mm_resid(merged, w_out, x, mvec(5), layer, scale=1.0, tm=tm, tn=512, rows_per_gate=rows_per_seq,
                  name=nm("mix_out"))

    x = ffn(x, 2, w_ffn2_in, w_ffn2_out)
    return x, kv, zx3, h_new


def kernel(x_prompt, x_sample, c_prompt, c_sample, cache_k, cache_v, page_table, state_ssm, state_conv, norm_g,
           w_mod, b_mod, w_ffn1_in, w_ffn1_out, w_in, sb_bias, conv_w, conv_b, dt_bias, a_log, d_skip, ssm_norm_g,
           w_gate, b_gate, w_branch_attn, w_branch_ssm, w_out, w_ffn2_in, w_ffn2_out, final_norm_g):
    bp, tp, d = x_prompt.shape
    bs, ts, _ = x_sample.shape
    depth = norm_g.shape[0]
    n_pool, page = cache_k.shape[1], cache_k.shape[2]
    d_ssm = w_branch_ssm.shape[1]
    n_heads = dt_bias.shape[1]
    conv_ch = conv_w.shape[2]
    n_main = 3 * D_ATTN + d_ssm + conv_ch

    xp = x_prompt.reshape(bp * tp, d)
    xs = x_sample.reshape(bs * ts, d)
    c_all = jnp.concatenate([c_prompt, c_sample], axis=0)
    c_rows = -(-c_all.shape[0] // SUBLANES) * SUBLANES
    c_all = jnp.pad(c_all, ((0, c_rows - c_all.shape[0]), (0, 0)))
    h0_prompt = jnp.zeros((bp, d_ssm, SSM_STATE), F32)
    conv0_prompt = jnp.zeros((bp, CONV_WIDTH - 1, conv_ch), F32)
    ck = cache_k.reshape(depth, n_pool, page * SB_HEADS, SB_HEAD_DIM)
    cv = cache_v.reshape(depth, n_pool, page * SB_HEADS, SB_HEAD_DIM)
    w_in_t = jnp.swapaxes(w_in, 1, 2)
    wts = (w_ffn1_in, w_ffn1_out, w_in_t, w_gate, w_branch_attn, w_branch_ssm, w_out, w_ffn2_in, w_ffn2_out)

    outs = {k: [] for k in ("kp", "vp", "hp", "cp", "ks", "vs", "hs", "cs")}
    for l in range(depth):
        mod = _mm_bias(c_all, w_mod, b_mod[l][None, :], l, col0=0, n=N_MOD * d, tm=c_rows, tn=512,
                       pre_silu=True, name=f"mod_l{l}").reshape(c_rows, N_MOD, d)
        small = (norm_g[l], conv_w[l], conv_b[l], dt_bias[l], a_log[l], d_skip[l], ssm_norm_g[l], b_gate[l])

        def attn_p(qkv):
            return _attn_prompt(qkv.reshape(bp, tp, 3 * D_ATTN), sb_bias[l]).reshape(bp * tp, D_ATTN)

        xp, kv_p, zx_p, h_p = _layer(xp, mod[:bp], attn_p, h0_prompt, conv0_prompt, wts, small, l,
                                     rows_per_seq=tp, tm=min(tp, 1024), tn_ffn=256, split_qkv=True, tag=f"p{l}")
        outs["kp"].append(kv_p[0].reshape(bp, tp, SB_HEADS, SB_HEAD_DIM))
        outs["vp"].append(kv_p[1].reshape(bp, tp, SB_HEADS, SB_HEAD_DIM))
        outs["hp"].append(h_p.reshape(bp, n_heads, SSM_HEAD_DIM, SSM_STATE))
        outs["cp"].append(zx_p[:, tp - (CONV_WIDTH - 1):, d_ssm:])

        def attn_s(qkv):
            return _attn_sample(qkv, ck, cv, page_table, sb_bias[l], l, tq=ts)

        xs, kv_s, zx_s, h_s = _layer(xs, mod[bp:bp + bs], attn_s, state_ssm[l].reshape(bs, d_ssm, SSM_STATE),
                                     state_conv[l], wts, small, l, rows_per_seq=ts, tm=bs * ts, tn_ffn=256,
                                     split_qkv=False, tag=f"s{l}")
        outs["ks"].append(kv_s[0].reshape(bs, ts, SB_HEADS, SB_HEAD_DIM))
        outs["vs"].append(kv_s[1].reshape(bs, ts, SB_HEADS, SB_HEAD_DIM))
        outs["hs"].append(h_s.reshape(bs, n_heads, SSM_HEAD_DIM, SSM_STATE))
        conv_in = jnp.concatenate([state_conv[l], zx_s[:, :ts, d_ssm:]], axis=1)
        outs["cs"].append(conv_in[:, -(CONV_WIDTH - 1):])

    y_prompt = _rmsnorm(xp, final_norm_g[None, :], tm=256, name="final_norm_p").reshape(bp, tp, d)
    y_sample = _rmsnorm(xs, final_norm_g[None, :], tm=bs * ts, name="final_norm_s").reshape(bs, ts, d)
    st = lambda k: jnp.stack(outs[k])
    return (y_prompt, y_sample, st("kp"), st("vp"), st("hp"), st("cp"), st("ks"), st("vs"), st("hs"), st("cs"))
```

```python
import functools

import jax
import jax.numpy as jnp
from jax import lax
from jax.experimental import pallas as pl
from jax.experimental.pallas import tpu as pltpu

F32 = jnp.float32
BF16 = jnp.bfloat16

SB_HEADS = 16
HEADS_SHIFT = 4
TQ_SHIFT = 3
SB_HEAD_DIM = 128
D_ATTN = SB_HEADS * SB_HEAD_DIM
SSM_HEAD_DIM = 64
HEAD_DIM_SHIFT = 6
SSM_GROUPS = 8
SSM_STATE = 128
CONV_WIDTH = 4
N_MOD = 9
EPS = 1e-6
CHUNK = 128
LANES = 128
SUBLANES = 8
VMEM_LIMIT = 56 * 1024 * 1024

ATTN_BLOCK = 256
ATTN_HEADS_PER_STEP = 16
PAGES_PER_STEP = 8
CONV_ROWS = 512
SSD_GROUPS_PER_STEP = 8


def _params(*sem):
    return pltpu.CompilerParams(dimension_semantics=sem, vmem_limit_bytes=VMEM_LIMIT)


def _wspec(block, imap, layer):
    return pl.BlockSpec((None,) + block, lambda *a: (layer,) + imap(*a))


def _sigmoid(x):
    return 1.0 / (1.0 + jnp.exp(-x))


def _silu(x):
    return x * _sigmoid(x)


def _softplus(x):
    return jnp.maximum(x, 0.0) + jnp.log1p(jnp.exp(-jnp.abs(x)))


def _bdot(a, b):
    return jnp.dot(a.astype(BF16), b.astype(BF16), preferred_element_type=F32)


def _bdot_nt(a, b):
    return lax.dot_general(a.astype(BF16), b.astype(BF16), (((1,), (1,)), ((), ())),
                           preferred_element_type=F32)


def _split3(x):
    hi = x.astype(BF16)
    r1 = x - hi.astype(F32)
    mid = r1.astype(BF16)
    lo = (r1 - mid.astype(F32)).astype(BF16)
    return hi, mid, lo


def _exact_dot(x, m01):
    hi, mid, lo = _split3(x)
    d = functools.partial(jnp.dot, preferred_element_type=F32)
    return d(hi, m01) + d(mid, m01) + d(lo, m01)


def _mm_bias_kernel(x_ref, w_ref, b_ref, o_ref, *, pre_silu, w_is_nk):
    x = x_ref[...]
    if pre_silu:
        x = _silu(x.astype(F32))
    o_ref[...] = (_bdot_nt if w_is_nk else _bdot)(x, w_ref[...]) + b_ref[...]


def _mm_bias(x, w, b, layer, *, col0, n, tm, tn, pre_silu=False, w_is_nk=False, name):
    m, k = x.shape
    c0 = col0 // tn
    w_spec = (_wspec((tn, k), lambda i, j: (c0 + j, 0), layer) if w_is_nk else
              _wspec((k, tn), lambda i, j: (0, c0 + j), layer))
    return pl.pallas_call(
        functools.partial(_mm_bias_kernel, pre_silu=pre_silu, w_is_nk=w_is_nk),
        grid=(m // tm, n // tn),
        in_specs=[pl.BlockSpec((tm, k), lambda i, j: (i, 0)),
                  w_spec,
                  pl.BlockSpec((1, tn), lambda i, j: (0, c0 + j))],
        out_specs=pl.BlockSpec((tm, tn), lambda i, j: (i, j)),
        out_shape=jax.ShapeDtypeStruct((m, n), F32),
        compiler_params=_params("parallel", "arbitrary"),
        name=name,
    )(x, w, b)


def _mm_qkv_kernel(x_ref, w_ref, *o_refs, nq, scale):
    j = pl.program_id(1)
    acc = _bdot_nt(x_ref[...], w_ref[...])
    o_refs[0][...] = (acc * jnp.where(j < nq, scale, 1.0)).astype(o_refs[0].dtype)
    if len(o_refs) > 1:
        @pl.when((j >= nq) & (j < 2 * nq))
        def _():
            o_refs[1][...] = acc

        @pl.when(j >= 2 * nq)
        def _():
            o_refs[2][...] = acc


def _mm_qkv(x, w, layer, *, tm, tn, scale, split, name):
    m, k = x.shape
    nq = D_ATTN // tn
    out_specs = [pl.BlockSpec((tm, tn), lambda i, j: (i, j))]
    out_shape = [jax.ShapeDtypeStruct((m, 3 * D_ATTN), BF16 if split else F32)]
    if split:
        for sec in (1, 2):
            out_specs.append(pl.BlockSpec(
                (tm, tn), functools.partial(lambda sec, i, j: (i, jnp.clip(j - sec * nq, 0, nq - 1)), sec)))
            out_shape.append(jax.ShapeDtypeStruct((m, D_ATTN), F32))
    outs = pl.pallas_call(
        functools.partial(_mm_qkv_kernel, nq=nq, scale=scale),
        grid=(m // tm, 3 * nq),
        in_specs=[pl.BlockSpec((tm, k), lambda i, j: (i, 0)),
                  _wspec((tn, k), lambda i, j: (j, 0), layer)],
        out_specs=out_specs,
        out_shape=out_shape,
        compiler_params=_params("parallel", "arbitrary"),
        name=name,
    )(x, w)
    return outs if split else outs[0]


def _mm_sigmoid_kernel(x_ref, w_ref, b_ref, o_ref):
    acc = _bdot(x_ref[...], w_ref[...])
    o_ref[...] = _sigmoid(acc + b_ref[...]).astype(o_ref.dtype)


def _mm_sigmoid(x, w, b, layer, *, tm, tn, name):
    m, k = x.shape
    n = w.shape[2]
    return pl.pallas_call(
        _mm_sigmoid_kernel,
        grid=(m // tm, n // tn),
        in_specs=[pl.BlockSpec((tm, k), lambda i, j: (i, 0)),
                  _wspec((k, tn), lambda i, j: (0, j), layer),
                  pl.BlockSpec((1, tn), lambda i, j: (0, j))],
        out_specs=pl.BlockSpec((tm, tn), lambda i, j: (i, j)),
        out_shape=jax.ShapeDtypeStruct((m, n), BF16),
        compiler_params=_params("parallel", "arbitrary"),
        name=name,
    )(x, w, b)


def _mm_swiglu_kernel(x_ref, wg_ref, wu_ref, o_ref):
    x = x_ref[...]
    gate = _bdot(x, wg_ref[...])
    up = _bdot(x, wu_ref[...])
    o_ref[...] = (_silu(gate) * up).astype(o_ref.dtype)


def _mm_swiglu(x, w, layer, *, tm, tn, name):
    m, k = x.shape
    f = w.shape[2] // 2
    nb = f // tn
    return pl.pallas_call(
        _mm_swiglu_kernel,
        grid=(m // tm, nb),
        in_specs=[pl.BlockSpec((tm, k), lambda i, j: (i, 0)),
                  _wspec((k, tn), lambda i, j: (0, j), layer),
                  _wspec((k, tn), lambda i, j: (0, nb + j), layer)],
        out_specs=pl.BlockSpec((tm, tn), lambda i, j: (i, j)),
        out_shape=jax.ShapeDtypeStruct((m, f), BF16),
        compiler_params=_params("parallel", "arbitrary"),
        name=name,
    )(x, w, w)


def _mm_resid_kernel(x_ref, w_ref, r_ref, g_ref, o_ref, *, scale):
    acc = _bdot(x_ref[...], w_ref[...])
    o_ref[...] = r_ref[...] + (scale * g_ref[0]) * acc


def _mm_resid(x, w, res, gate, layer, *, scale, tm, tn, rows_per_gate, name, single_buffer_x=False):
    m, k = x.shape
    n = w.shape[2]
    r = gate.shape[1]
    if r == 1:
        assert rows_per_gate % tm == 0
        gmap = lambda i, j: ((i * tm) // rows_per_gate, 0, j)
    else:
        assert r == tm
        gmap = lambda i, j: (i, 0, j)
    x_mode = dict(pipeline_mode=pl.Buffered(1)) if single_buffer_x else {}
    return pl.pallas_call(
        functools.partial(_mm_resid_kernel, scale=scale),
        grid=(m // tm, n // tn),
        in_specs=[pl.BlockSpec((tm, k), lambda i, j: (i, 0), **x_mode),
                  _wspec((k, tn), lambda i, j: (0, j), layer),
                  pl.BlockSpec((tm, tn), lambda i, j: (i, j)),
                  pl.BlockSpec((1, r, tn), gmap)],
        out_specs=pl.BlockSpec((tm, tn), lambda i, j: (i, j)),
        out_shape=jax.ShapeDtypeStruct((m, n), F32),
        compiler_params=_params("parallel", "arbitrary"),
        name=name,
    )(x, w, res, gate)


def _mm_merge_kernel(oa_ref, wa_ref, os_ref, ws_ref, ga_ref, gs_ref, o_ref):
    a = _bdot(oa_ref[...], wa_ref[...])
    s = _bdot(os_ref[...], ws_ref[...])
    o_ref[...] = (ga_ref[...].astype(F32) * a + gs_ref[...].astype(F32) * s).astype(o_ref.dtype)


def _mm_merge(oa, wa, os_, ws, gates, layer, *, tm, tn, name):
    m = oa.shape[0]
    n = wa.shape[2]
    nb = n // tn
    return pl.pallas_call(
        _mm_merge_kernel,
        grid=(m // tm, nb),
        in_specs=[pl.BlockSpec((tm, oa.shape[1]), lambda i, j: (i, 0)),
                  _wspec((wa.shape[1], tn), lambda i, j: (0, j), layer),
                  pl.BlockSpec((tm, os_.shape[1]), lambda i, j: (i, 0)),
                  _wspec((ws.shape[1], tn), lambda i, j: (0, j), layer),
                  pl.BlockSpec((tm, tn), lambda i, j: (i, j)),
                  pl.BlockSpec((tm, tn), lambda i, j: (i, nb + j))],
        out_specs=pl.BlockSpec((tm, tn), lambda i, j: (i, j)),
        out_shape=jax.ShapeDtypeStruct((m, n), BF16),
        compiler_params=_params("parallel", "arbitrary"),
        name=name,
    )(oa, wa, os_, ws, gates, gates)


def _modnorm_kernel(x_ref, g_ref, sh_ref, sc_ref, o_ref):
    x = x_ref[...]
    y = x * lax.rsqrt(jnp.mean(x * x, axis=-1, keepdims=True) + EPS)
    y = y * g_ref[...]
    o_ref[...] = (y * (1.0 + sc_ref[0]) + sh_ref[0]).astype(o_ref.dtype)


def _modnorm(x, g, shift, scale, *, rows_per_seq, tm, name):
    m, d = x.shape
    r = shift.shape[1]
    if r == 1:
        assert rows_per_seq % tm == 0
        smap = lambda i: ((i * tm) // rows_per_seq, 0, 0)
    else:
        assert r == tm
        smap = lambda i: (i, 0, 0)
    return pl.pallas_call(
        _modnorm_kernel,
        grid=(m // tm,),
        in_specs=[pl.BlockSpec((tm, d), lambda i: (i, 0)),
                  pl.BlockSpec((1, d), lambda i: (0, 0)),
                  pl.BlockSpec((1, r, d), smap),
                  pl.BlockSpec((1, r, d), smap)],
        out_specs=pl.BlockSpec((tm, d), lambda i: (i, 0)),
        out_shape=jax.ShapeDtypeStruct((m, d), BF16),
        compiler_params=_params("parallel"),
        name=name,
    )(x, g, shift, scale)


def _rmsnorm_kernel(x_ref, g_ref, o_ref):
    x = x_ref[...]
    y = x * lax.rsqrt(jnp.mean(x * x, axis=-1, keepdims=True) + EPS)
    o_ref[...] = y * g_ref[...]


def _rmsnorm(x, g, *, tm, name):
    m, d = x.shape
    return pl.pallas_call(
        _rmsnorm_kernel,
        grid=(m // tm,),
        in_specs=[pl.BlockSpec((tm, d), lambda i: (i, 0)),
                  pl.BlockSpec((1, d), lambda i: (0, 0))],
        out_specs=pl.BlockSpec((tm, d), lambda i: (i, 0)),
        out_shape=jax.ShapeDtypeStruct((m, d), F32),
        compiler_params=_params("parallel"),
        name=name,
    )(x, g)


def _cumsum_rhs():
    j = lax.broadcasted_iota(jnp.int32, (2 * LANES, 2 * LANES), 0) & (LANES - 1)
    s = lax.broadcasted_iota(jnp.int32, (2 * LANES, 2 * LANES), 1)
    return jnp.where((j > s) | (s >= LANES), 1.0, 0.0).astype(BF16)


def _sb_weights(z_tiles, vis_tiles, carry, rhs):
    ls_t, part_t = [], []
    for z, vis in zip(z_tiles, vis_tiles):
        t = jnp.log(1.0 + jnp.exp(-jnp.abs(z)))
        ls_t.append(jnp.minimum(z, 0.0) - t)
        lk = -jnp.maximum(z, 0.0) - t
        if vis is not None:
            lk = jnp.where(vis, lk, 0.0)
        hi = lk.astype(BF16)
        lo = (lk - hi.astype(F32)).astype(BF16)
        part_t.append(jnp.dot(jnp.concatenate([hi, lo], axis=1), rhs, preferred_element_type=F32))
    w_t = [None] * len(z_tiles)
    for i in range(len(z_tiles) - 1, -1, -1):
        later = carry + part_t[i][:, :LANES]
        w = jnp.exp(ls_t[i] + later)
        if vis_tiles[i] is not None:
            w = jnp.where(vis_tiles[i], w, 0.0)
        w_t[i] = w
        carry = carry + part_t[i][:, LANES:]
    return w_t, carry


def _attn_prompt_kernel(bias_ref, q_ref, k_ref, v_ref, o_ref, acc_ref, carry_ref):
    hg = pl.program_id(1)
    qi = pl.program_id(2)
    nh = ATTN_HEADS_PER_STEP
    nt = ATTN_BLOCK // LANES
    rhs = _cumsum_rhs()
    row = lax.broadcasted_iota(jnp.int32, (ATTN_BLOCK, LANES), 0)
    col = lax.broadcasted_iota(jnp.int32, (ATTN_BLOCK, LANES), 1)
    diag_vis = [col + t * LANES < row for t in range(nt)]

    def hcols(h):
        return slice(h * SB_HEAD_DIM, (h + 1) * SB_HEAD_DIM)

    def block(kb, vis_tiles, first):
        start = pl.multiple_of(kb * ATTN_BLOCK, ATTN_BLOCK)
        for h in range(nh):
            bias = bias_ref[hg * nh + h]
            q = q_ref[0, :, hcols(h)]
            kblk = k_ref[0, pl.ds(start, ATTN_BLOCK), hcols(h)]
            vblk = v_ref[0, pl.ds(start, ATTN_BLOCK), hcols(h)]
            z = _bdot_nt(q, kblk) + bias
            carry = jnp.zeros((ATTN_BLOCK, LANES), F32) if first else carry_ref[h]
            w_t, carry = _sb_weights([z[:, t * LANES:(t + 1) * LANES] for t in range(nt)], vis_tiles, carry, rhs)
            carry_ref[h] = carry
            pv = jnp.dot(jnp.concatenate([w.astype(BF16) for w in w_t], axis=1), vblk, preferred_element_type=F32)
            acc_ref[h] = pv if first else acc_ref[h] + pv

    block(qi, diag_vis, True)

    def body(it, _):
        block(qi - 1 - it, [None] * nt, False)
        return 0

    lax.fori_loop(0, qi, body, 0)
    for h in range(nh):
        o_ref[0, :, hcols(h)] = acc_ref[h].astype(o_ref.dtype)


def _attn_prompt(qkv, sb_bias):
    b, t, _ = qkv.shape
    nq = t // ATTN_BLOCK
    nh = ATTN_HEADS_PER_STEP
    w = nh * SB_HEAD_DIM
    ng = SB_HEADS // nh
    return pl.pallas_call(
        _attn_prompt_kernel,
        grid=(b, ng, nq),
        in_specs=[pl.BlockSpec(memory_space=pltpu.SMEM),
                  pl.BlockSpec((1, ATTN_BLOCK, w), lambda bi, h, qi: (bi, qi, h)),
                  pl.BlockSpec((1, t, w), lambda bi, h, qi: (bi, 0, ng + h)),
                  pl.BlockSpec((1, t, w), lambda bi, h, qi: (bi, 0, 2 * ng + h))],
        out_specs=pl.BlockSpec((1, ATTN_BLOCK, w), lambda bi, h, qi: (bi, qi, h)),
        out_shape=jax.ShapeDtypeStruct((b, t, D_ATTN), BF16),
        scratch_shapes=[pltpu.VMEM((nh, ATTN_BLOCK, SB_HEAD_DIM), F32),
                        pltpu.VMEM((nh, ATTN_BLOCK, LANES), F32)],
        compiler_params=_params("parallel", "parallel", "arbitrary"),
        name="attn_prompt",
    )(sb_bias, qkv, qkv, qkv)


def _attn_sample_kernel(pt_ref, bias_ref, q_ref, kn_ref, vn_ref, *refs, tq):
    npg = PAGES_PER_STEP
    kc_refs, vc_refs = refs[:npg], refs[npg:2 * npg]
    o_ref, acc_ref, carry_ref, spread_ref, own_ref = refs[2 * npg:]
    s = pl.program_id(1)
    rows = SB_HEADS * tq
    page_rows = CHUNK * SB_HEADS
    rhs = _cumsum_rhs()

    def hcols(h):
        return slice(h * SB_HEAD_DIM, (h + 1) * SB_HEAD_DIM)

    def process(k_tiles, pv, vis):
        z_t = [jnp.concatenate([_bdot_nt(q_ref[:, hcols(h)], k_of(h)) + bias_ref[h] for h in range(SB_HEADS)], axis=0)
               for k_of in k_tiles]
        w_t, carry = _sb_weights(z_t, [vis] * len(z_t), carry_ref[...], rhs)
        carry_ref[...] = carry
        pv(w_t)

    @pl.when(s == 0)
    def _():
        carry_ref[...] = jnp.zeros_like(carry_ref)
        acc_ref[...] = jnp.zeros_like(acc_ref)
        tok = lax.broadcasted_iota(jnp.int32, (rows, page_rows), 0)
        r_i = lax.broadcasted_iota(jnp.int32, (rows, page_rows), 1)
        spread_ref[...] = jnp.where(lax.shift_right_logical(r_i, HEADS_SHIFT) == tok, 1.0, 0.0).astype(BF16)
        own_ref[...] = jnp.where((r_i & (SB_HEADS - 1)) == lax.shift_right_logical(tok, TQ_SHIFT), 1.0, 0.0).astype(BF16)

        pad = jnp.zeros((CHUNK - tq, SB_HEAD_DIM), F32)
        r_idx = lax.broadcasted_iota(jnp.int32, (rows, CHUNK), 0) & (tq - 1)
        s_idx = lax.broadcasted_iota(jnp.int32, (rows, CHUNK), 1)

        def pv_new(w_t):
            for h in range(SB_HEADS):
                v_h = jnp.concatenate([vn_ref[:, hcols(h)], pad], axis=0)
                acc_ref[h * tq:(h + 1) * tq, :] += _bdot(w_t[0][h * tq:(h + 1) * tq, :], v_h)

        process([lambda h: jnp.concatenate([kn_ref[:, hcols(h)], pad], axis=0)], pv_new, s_idx < r_idx)

    def head_rows(r, h):
        return r[pl.ds(h, CHUNK, stride=SB_HEADS), :]

    def pv_pages(w_t):
        upd = jnp.zeros((rows, SB_HEAD_DIM), F32)
        for w, v_ref in zip(w_t, reversed(vc_refs)):
            wide = jnp.dot(w.astype(BF16), spread_ref[...], preferred_element_type=F32)
            upd = upd + jnp.dot(wide.astype(BF16) * own_ref[...], v_ref[...].astype(BF16),
                                preferred_element_type=F32)
        acc_ref[...] += upd

    process([functools.partial(head_rows, r) for r in reversed(kc_refs)], pv_pages, None)

    @pl.when(s == pl.num_programs(1) - 1)
    def _():
        for h in range(SB_HEADS):
            o_ref[:, hcols(h)] = acc_ref[h * tq:(h + 1) * tq, :].astype(o_ref.dtype)


def _attn_sample(qkv, cache_k, cache_v, page_table, sb_bias, layer, *, tq):
    bs, n_pages = page_table.shape
    page_rows = cache_k.shape[2]
    npg = PAGES_PER_STEP
    assert page_rows == CHUNK * SB_HEADS and tq == SUBLANES == 1 << TQ_SHIFT and n_pages % npg == 0
    new_specs = [pl.BlockSpec((tq, D_ATTN), functools.partial(lambda sec, b, s, pt: (b, sec), sec))
                 for sec in range(3)]

    def page_spec(i):
        return pl.BlockSpec((None, None, page_rows, SB_HEAD_DIM),
                            lambda b, s, pt: (layer, pt[b, n_pages - 1 - (s * npg + i)], 0, 0))

    return pl.pallas_call(
        functools.partial(_attn_sample_kernel, tq=tq),
        grid_spec=pltpu.PrefetchScalarGridSpec(
            num_scalar_prefetch=1,
            grid=(bs, n_pages // npg),
            in_specs=[pl.BlockSpec(memory_space=pltpu.SMEM)] + new_specs + [page_spec(i) for i in range(npg)] * 2,
            out_specs=pl.BlockSpec((tq, D_ATTN), lambda b, s, pt: (b, 0)),
            scratch_shapes=[pltpu.VMEM((SB_HEADS * tq, SB_HEAD_DIM), F32),
                            pltpu.VMEM((SB_HEADS * tq, LANES), F32),
                            pltpu.VMEM((SB_HEADS * tq, page_rows), BF16),
                            pltpu.VMEM((SB_HEADS * tq, page_rows), BF16)],
        ),
        out_shape=jax.ShapeDtypeStruct((bs * tq, D_ATTN), F32),
        compiler_params=_params("parallel", "arbitrary"),
        name="attn_sample",
    )(page_table, sb_bias, qkv, qkv, qkv, *([cache_k] * npg), *([cache_v] * npg))


def _conv_kernel(cur_ref, prev_ref, buf_ref, w_ref, b_ref, o_ref, xp_ref):
    c = pl.program_id(1)
    rows = cur_ref.shape[1]
    prev = jnp.where(c == 0, buf_ref[0], prev_ref[0])
    xp_ref[0:SUBLANES, :] = prev
    xp_ref[SUBLANES:, :] = cur_ref[0]
    out = b_ref[...] + xp_ref[pl.ds(SUBLANES - CONV_WIDTH + 1, rows), :] * w_ref[0:1, :]
    for i in range(1, CONV_WIDTH):
        out = out + xp_ref[pl.ds(SUBLANES - CONV_WIDTH + 1 + i, rows), :] * w_ref[i:i + 1, :]
    o_ref[0] = _silu(out)


def _conv(zx, buf8, conv_w, conv_b, *, col0, tc, name):
    b, t, _ = zx.shape
    ch = conv_w.shape[1]
    c0 = col0 // tc
    rows = min(t, CONV_ROWS)
    per = rows // SUBLANES
    return pl.pallas_call(
        _conv_kernel,
        grid=(b, t // rows, ch // tc),
        in_specs=[pl.BlockSpec((1, rows, tc), lambda bi, c, j: (bi, c, c0 + j)),
                  pl.BlockSpec((1, SUBLANES, tc), lambda bi, c, j: (bi, jnp.maximum(c * per - 1, 0), c0 + j)),
                  pl.BlockSpec((1, SUBLANES, tc), lambda bi, c, j: (bi, 0, j)),
                  pl.BlockSpec((CONV_WIDTH, tc), lambda bi, c, j: (0, j)),
                  pl.BlockSpec((1, tc), lambda bi, c, j: (0, j))],
        out_specs=pl.BlockSpec((1, rows, tc), lambda bi, c, j: (bi, c, j)),
        out_shape=jax.ShapeDtypeStruct((b, t, ch), F32),
        scratch_shapes=[pltpu.VMEM((rows + SUBLANES, tc), F32)],
        compiler_params=_params("parallel", "parallel", "parallel"),
        name=name,
    )(zx, zx, buf8, conv_w, conv_b)


def _ssd_kernel(xs_ref, bm_ref, cm_ref, z_ref, dt_ref, dtt_ref, h0_ref, dtb_row_ref, dtb_col_ref, alog_row_ref,
                alog_col_ref, dskip_ref, ng_ref, y_ref, hout_ref, ht_ref, *, t_valid):
    c = pl.program_id(2)
    gps = bm_ref.shape[2] // SSM_STATE
    gw = xs_ref.shape[2] // gps
    hpg = gw // SSM_HEAD_DIM
    f32dot = functools.partial(jnp.dot, preferred_element_type=F32)

    @pl.when(c == 0)
    def _():
        ht_ref[...] = h0_ref[0].T

    row = lax.broadcasted_iota(jnp.int32, (CHUNK, CHUNK), 0)
    col = lax.broadcasted_iota(jnp.int32, (CHUNK, CHUNK), 1)
    causal = row >= col
    tri_incl = jnp.where(causal, 1.0, 0.0).astype(BF16)
    tri_incl_t = jnp.where(col >= row, 1.0, 0.0).astype(BF16)
    e_row = lax.broadcasted_iota(jnp.int32, (LANES, gw), 0)
    e_col = lax.broadcasted_iota(jnp.int32, (LANES, gw), 1)
    t_col = c * CHUNK + lax.broadcasted_iota(jnp.int32, (CHUNK, gw), 0)
    t_row = c * CHUNK + lax.broadcasted_iota(jnp.int32, (hpg, CHUNK), 1)
    lane = lax.broadcasted_iota(jnp.int32, (CHUNK, LANES), 1)
    first_head = lane < SSM_HEAD_DIM
    dt_heads = jnp.where(lane < SSM_GROUPS * hpg, dt_ref[0], 0.0)

    for u in range(gps):
        g = pl.program_id(1) * gps + u
        ch = slice(u * gw, (u + 1) * gw)
        st = slice(u * SSM_STATE, (u + 1) * SSM_STATE)

        expand = jnp.where(e_row == g * hpg + lax.shift_right_logical(e_col, HEAD_DIM_SHIFT), 1.0, 0.0).astype(BF16)

        dt_raw = _exact_dot(dt_heads, expand)
        dt = jnp.where(t_col < t_valid, _softplus(dt_raw + dtb_row_ref[:, ch]), 0.0)
        a_row = -jnp.exp(alog_row_ref[:, ch])
        hi, mid, lo = _split3(dt * a_row)
        a_cum = f32dot(tri_incl, hi) + f32dot(tri_incl, mid) + f32dot(tri_incl, lo)
        a_end = a_cum[CHUNK - 1:CHUNK, :]

        h_lo = pl.multiple_of(g * hpg, hpg)
        dt_r = jnp.where(t_row < t_valid,
                         _softplus(dtt_ref[0, pl.ds(h_lo, hpg), :] + dtb_col_ref[pl.ds(h_lo, hpg), :]), 0.0)
        hi, mid, lo = _split3(dt_r * (-jnp.exp(alog_col_ref[pl.ds(h_lo, hpg), :])))
        a_cum_r = f32dot(hi, tri_incl_t) + f32dot(mid, tri_incl_t) + f32dot(lo, tri_incl_t)

        xs = xs_ref[0, :, ch]
        bm = bm_ref[0, :, st]
        cm = cm_ref[0, :, st].astype(BF16)
        xdt = xs * dt
        xdt_b = xdt.astype(BF16)
        cb = _bdot_nt(cm, bm)

        ht = ht_ref[:, ch]
        y = f32dot(cm, ht.astype(BF16)) * jnp.exp(a_cum)
        y_diag = []
        for p in range(hpg // 2):
            cols = slice(p * LANES, (p + 1) * LANES)
            scores = []
            for j in (2 * p, 2 * p + 1):
                a_l = jnp.broadcast_to(a_cum[:, j * SSM_HEAD_DIM:j * SSM_HEAD_DIM + 1], (CHUNK, CHUNK))
                seg = a_l - a_cum_r[j:j + 1, :]
                scores.append((cb * jnp.exp(jnp.where(causal, seg, -jnp.inf))).astype(BF16))
            xp = xdt_b[:, cols]
            rhs = jnp.concatenate([jnp.where(first_head, xp, jnp.zeros_like(xp)),
                                   jnp.where(first_head, jnp.zeros_like(xp), xp)], axis=0)
            y_diag.append(f32dot(jnp.concatenate(scores, axis=1), rhs))
        y = y + jnp.concatenate(y_diag, axis=1)

        xd = (xdt * jnp.exp(a_end - a_cum)).astype(BF16)
        ht_ref[:, ch] = ht * jnp.exp(a_end) + f32dot(bm.T.astype(BF16), xd)

        y = y + xs * dskip_ref[:, ch]
        y = y * _silu(z_ref[0, :, ch])
        y = y * lax.rsqrt(jnp.mean(y * y, axis=-1, keepdims=True) + EPS)
        y_ref[0, :, ch] = (y * ng_ref[:, ch]).astype(y_ref.dtype)

    @pl.when(c == pl.num_programs(2) - 1)
    def _():
        hout_ref[0] = ht_ref[...].T


def _ssd(xc, zx, dt, dtt, h0, dtb_row, dtb_col, alog_row, alog_col, dskip, ng, *, t_valid, name):
    b, t, cc = xc.shape
    d_ssm = cc - 2 * SSM_GROUPS * SSM_STATE
    gps = SSD_GROUPS_PER_STEP
    gw = gps * d_ssm // SSM_GROUPS
    sw = gps * SSM_STATE
    nb0 = d_ssm // sw
    assert d_ssm % sw == 0 and SSM_GROUPS % gps == 0
    gmap = lambda bi, g, c: (0, g)
    return pl.pallas_call(
        functools.partial(_ssd_kernel, t_valid=t_valid),
        grid=(b, SSM_GROUPS // gps, t // CHUNK),
        in_specs=[pl.BlockSpec((1, CHUNK, gw), lambda bi, g, c: (bi, c, g)),
                  pl.BlockSpec((1, CHUNK, sw), lambda bi, g, c: (bi, c, nb0 + g)),
                  pl.BlockSpec((1, CHUNK, sw), lambda bi, g, c: (bi, c, nb0 + SSM_GROUPS // gps + g)),
                  pl.BlockSpec((1, CHUNK, gw), lambda bi, g, c: (bi, c, g)),
                  pl.BlockSpec((1, CHUNK, LANES), lambda bi, g, c: (bi, c, 0)),
                  pl.BlockSpec((1, LANES, CHUNK), lambda bi, g, c: (bi, 0, c)),
                  pl.BlockSpec((1, gw, SSM_STATE), lambda bi, g, c: (bi, g, 0)),
                  pl.BlockSpec((1, gw), gmap),
                  pl.BlockSpec((LANES, 1), lambda bi, g, c: (0, 0)),
                  pl.BlockSpec((1, gw), gmap),
                  pl.BlockSpec((LANES, 1), lambda bi, g, c: (0, 0)),
                  pl.BlockSpec((1, gw), gmap),
                  pl.BlockSpec((1, gw), gmap)],
        out_specs=[pl.BlockSpec((1, CHUNK, gw), lambda bi, g, c: (bi, c, g)),
                   pl.BlockSpec((1, gw, SSM_STATE), lambda bi, g, c: (bi, g, 0))],
        out_shape=[jax.ShapeDtypeStruct((b, t, d_ssm), BF16),
                   jax.ShapeDtypeStruct((b, d_ssm, SSM_STATE), F32)],
        scratch_shapes=[pltpu.VMEM((SSM_STATE, gw), F32)],
        compiler_params=_params("parallel", "parallel", "arbitrary"),
        name=name,
    )(xc, xc, xc, zx, dt, dtt, h0, dtb_row, dtb_col, alog_row, alog_col, dskip, ng)


def _pad_heads(v):
    return jnp.pad(v, (0, LANES - v.shape[0]))


def _layer(x, mod, attn_fn, h0, conv_buf, wts, small, layer, *, rows_per_seq, tm, tn_ffn, split_qkv, tag):
    m, d = x.shape
    nseq = m // rows_per_seq
    (w_ffn1_in, w_ffn1_out, w_in_t, w_gate, w_branch_attn, w_branch_ssm, w_out, w_ffn2_in, w_ffn2_out) = wts
    (norm_g, conv_w, conv_b, dt_bias, a_log, d_skip, ssm_norm_g, b_gate) = small
    per_seq = rows_per_seq % tm == 0
    tnorm = 256 if per_seq else m
    big_ffn_out = tm * w_ffn1_out.shape[1] * 2 > 16 * 1024 * 1024
    d_ssm = w_branch_ssm.shape[1]
    conv_ch = conv_w.shape[1]
    rep = lambda v: jnp.repeat(v, SSM_HEAD_DIM)[None, :]
    nm = lambda s: f"{s}_{tag}"

    def mvec(idx):
        v = mod[:, idx:idx + 1, :]
        return v if per_seq else jnp.repeat(v[:, 0, :], rows_per_seq, axis=0)[None]

    def modnorm(x, k):
        return _modnorm(x, norm_g[k:k + 1], mvec(3 * k), mvec(3 * k + 1), rows_per_seq=rows_per_seq, tm=tnorm,
                        name=nm(f"modnorm{k}"))

    def ffn(x, k, w_a, w_b):
        hdn = _mm_swiglu(modnorm(x, k), w_a, layer, tm=tm, tn=tn_ffn, name=nm(f"ffn{k}_in"))
        return _mm_resid(hdn, w_b, x, mvec(3 * k + 2), layer, scale=0.5, tm=tm, tn=tn_ffn,
                         rows_per_gate=rows_per_seq, name=nm(f"ffn{k}_out"), single_buffer_x=big_ffn_out)

    x = ffn(x, 0, w_ffn1_in, w_ffn1_out)

    n = modnorm(x, 1)
    zero_bias = jnp.zeros((1, w_in_t.shape[1]), F32)
    qkv = _mm_qkv(n, w_in_t, layer, tm=tm, tn=512, scale=SB_HEAD_DIM ** -0.5, split=split_qkv, name=nm("proj_qkv"))
    zx = _mm_bias(n, w_in_t, zero_bias, layer, col0=3 * D_ATTN, n=d_ssm + conv_ch, tm=tm, tn=512, w_is_nk=True,
                  name=nm("proj_zx"))
    dt_raw = _mm_bias(n, w_in_t, zero_bias, layer, col0=3 * D_ATTN + d_ssm + conv_ch, n=LANES, tm=tm, tn=LANES,
                      w_is_nk=True, name=nm("proj_dt"))
    gates = _mm_sigmoid(n, w_gate, b_gate[None, :], layer, tm=tm, tn=512, name=nm("gates"))

    o_attn = attn_fn(qkv[0] if split_qkv else qkv)
    kv = qkv[1:] if split_qkv else (qkv[:, D_ATTN:2 * D_ATTN], qkv[:, 2 * D_ATTN:])

    t_pad = -(-rows_per_seq // CHUNK) * CHUNK
    zx3 = zx.reshape(nseq, rows_per_seq, d_ssm + conv_ch)
    dt3 = dt_raw.reshape(nseq, rows_per_seq, LANES)
    if t_pad != rows_per_seq:
        padt = ((0, 0), (0, t_pad - rows_per_seq), (0, 0))
        zx3, dt3 = jnp.pad(zx3, padt), jnp.pad(dt3, padt)
    buf8 = jnp.pad(conv_buf, ((0, 0), (SUBLANES - conv_buf.shape[1], 0), (0, 0)))
    xc = _conv(zx3, buf8, conv_w, conv_b[None, :], col0=d_ssm, tc=1024, name=nm("conv"))
    y_ssm, h_new = _ssd(xc, zx3, dt3, jnp.swapaxes(dt3, 1, 2), h0,
                        rep(dt_bias), _pad_heads(dt_bias)[:, None], rep(a_log), _pad_heads(a_log)[:, None],
                        rep(d_skip), ssm_norm_g[None, :], t_valid=rows_per_seq, name=nm("ssd"))
    o_ssm = y_ssm[:, :rows_per_seq].reshape(m, d_ssm)

    merged = _mm_merge(o_attn, w_branch_attn, o_ssm, w_branch_ssm, gates, layer, tm=tm, tn=256, name=nm("merge"))
    x = _mm_resid(merged, w_out, x, mvec(5), layer, scale=1.0, tm=tm, tn=512, rows_per_gate=rows_per_seq,
                  name=nm("mix_out"))

    x = ffn(x, 2, w_ffn2_in, w_ffn2_out)
    return x, kv, zx3, h_new


def kernel(x_prompt, x_sample, c_prompt, c_sample, cache_k, cache_v, page_table, state_ssm, state_conv, norm_g,
           w_mod, b_mod, w_ffn1_in, w_ffn1_out, w_in, sb_bias, conv_w, conv_b, dt_bias, a_log, d_skip, ssm_norm_g,
           w_gate, b_gate, w_branch_attn, w_branch_ssm, w_out, w_ffn2_in, w_ffn2_out, final_norm_g):
    bp, tp, d = x_prompt.shape
    bs, ts, _ = x_sample.shape
    depth = norm_g.shape[0]
    n_pool, page = cache_k.shape[1], cache_k.shape[2]
    d_ssm = w_branch_ssm.shape[1]
    n_heads = dt_bias.shape[1]
    conv_ch = conv_w.shape[2]
    n_main = 3 * D_ATTN + d_ssm + conv_ch

    xp = x_prompt.reshape(bp * tp, d)
    xs = x_sample.reshape(bs * ts, d)
    c_all = jnp.concatenate([c_prompt, c_sample], axis=0)
    c_rows = -(-c_all.shape[0] // SUBLANES) * SUBLANES
    c_all = jnp.pad(c_all, ((0, c_rows - c_all.shape[0]), (0, 0)))
    h0_prompt = jnp.zeros((bp, d_ssm, SSM_STATE), F32)
    conv0_prompt = jnp.zeros((bp, CONV_WIDTH - 1, conv_ch), F32)
    ck = cache_k.reshape(depth, n_pool, page * SB_HEADS, SB_HEAD_DIM)
    cv = cache_v.reshape(depth, n_pool, page * SB_HEADS, SB_HEAD_DIM)
    w_in_t = jnp.swapaxes(w_in, 1, 2)
    wts = (w_ffn1_in, w_ffn1_out, w_in_t, w_gate, w_branch_attn, w_branch_ssm, w_out, w_ffn2_in, w_ffn2_out)

    outs = {k: [] for k in ("kp", "vp", "hp", "cp", "ks", "vs", "hs", "cs")}
    for l in range(depth):
        mod = _mm_bias(c_all, w_mod, b_mod[l][None, :], l, col0=0, n=N_MOD * d, tm=c_rows, tn=512,
                       pre_silu=True, name=f"mod_l{l}").reshape(c_rows, N_MOD, d)
        small = (norm_g[l], conv_w[l], conv_b[l], dt_bias[l], a_log[l], d_skip[l], ssm_norm_g[l], b_gate[l])

        def attn_p(qkv):
            return _attn_prompt(qkv.reshape(bp, tp, 3 * D_ATTN), sb_bias[l]).reshape(bp * tp, D_ATTN)

        xp, kv_p, zx_p, h_p = _layer(xp, mod[:bp], attn_p, h0_prompt, conv0_prompt, wts, small, l,
                                     rows_per_seq=tp, tm=min(tp, 1024), tn_ffn=256, split_qkv=True, tag=f"p{l}")
        outs["kp"].append(kv_p[0].reshape(bp, tp, SB_HEADS, SB_HEAD_DIM))
        outs["vp"].append(kv_p[1].reshape(bp, tp, SB_HEADS, SB_HEAD_DIM))
        outs["hp"].append(h_p.reshape(bp, n_heads, SSM_HEAD_DIM, SSM_STATE))
        outs["cp"].append(zx_p[:, tp - (CONV_WIDTH - 1):, d_ssm:])

        def attn_s(qkv):
            return _attn_sample(qkv, ck, cv, page_table, sb_bias[l], l, tq=ts)

        xs, kv_s, zx_s, h_s = _layer(xs, mod[bp:bp + bs], attn_s, state_ssm[l].reshape(bs, d_ssm, SSM_STATE),
                                     state_conv[l], wts, small, l, rows_per_seq=ts, tm=bs * ts, tn_ffn=256,
                                     split_qkv=False, tag=f"s{l}")
        outs["ks"].append(kv_s[0].reshape(bs, ts, SB_HEADS, SB_HEAD_DIM))
        outs["vs"].append(kv_s[1].reshape(bs, ts, SB_HEADS, SB_HEAD_DIM))
        outs["hs"].append(h_s.reshape(bs, n_heads, SSM_HEAD_DIM, SSM_STATE))
        conv_in = jnp.concatenate([state_conv[l], zx_s[:, :ts, d_ssm:]], axis=1)
        outs["cs"].append(conv_in[:, -(CONV_WIDTH - 1):])

    y_prompt = _rmsnorm(xp, final_norm_g[None, :], tm=256, name="final_norm_p").reshape(bp, tp, d)
    y_sample = _rmsnorm(xs, final_norm_g[None, :], tm=bs * ts, name="final_norm_s").reshape(bs, ts, d)
    st = lambda k: jnp.stack(outs[k])
    return (y_prompt, y_sample, st("kp"), st("vp"), st("hp"), st("cp"), st("ks"), st("vs"), st("hs"), st("cs"))
```
